```python
import math
import jax, jax.numpy as jnp
from jax import lax
import numpy as np

D_MODEL = 4096
BATCH = 2
SEQ = 8192
DEPTH = 2

D_MIX = D_MODEL
ATT_WIDTH = D_MIX // 2
ATT_HEAD_DIM = 128
ATT_HEADS = ATT_WIDTH // (2 * ATT_HEAD_DIM)
Q_BLOCK = 128
SSM_WIDTH = D_MIX - ATT_WIDTH
SSM_HEAD_DIM = 64
SSM_HEADS = SSM_WIDTH // SSM_HEAD_DIM
SSM_GROUPS = 8
SSM_HEADS_PER_GROUP = SSM_HEADS // SSM_GROUPS
SSM_STATE = 128
CONV_K = 4
CONV_CH = SSM_WIDTH + 2 * SSM_GROUPS * SSM_STATE
SSM_CHUNK = 128
IN_COLS = 3 * ATT_WIDTH + SSM_WIDTH + CONV_CH + SSM_HEADS
D_FF = 11008
N_EXPERTS = 8
TOP_K = 2
MOE_D_FF = 5632
N_DENSE = (DEPTH + 1) // 2
N_MOE = DEPTH // 2
EPS = 1e-6

kernel_name = "hybrid_diffattn_ssd_moe_trunk"


def rms_norm(x, w):
    xf = x.astype(jnp.float32)
    out = xf * lax.rsqrt(jnp.mean(xf * xf, axis=-1, keepdims=True) + EPS)
    return (out * w.astype(jnp.float32)).astype(x.dtype)


def swiglu(x, w_gate, w_up, w_down):
    return (jax.nn.silu(x @ w_gate) * (x @ w_up)) @ w_down


def lambda_init_for(layer):
    return 0.8 - 0.6 * math.exp(-0.3 * layer)


def diff_attention(q, k, v, q_norm_w, k_norm_w, lam, lambda_init, subln_w):
    b, s = q.shape[:2]
    q = rms_norm(q, q_norm_w)
    k = rms_norm(k, k_norm_w)
    scale = ATT_HEAD_DIM ** -0.5
    n_blk = s // Q_BLOCK
    q_blocks = jnp.moveaxis(q.reshape(b, n_blk, Q_BLOCK, ATT_HEADS, 2, ATT_HEAD_DIM), 1, 0)
    k_pos = jnp.arange(s)

    def attend_block(args):
        qb, blk = args
        scores = jnp.einsum('bqhcd,bkhcd->bhcqk', qb, k).astype(jnp.float32) * scale
        q_pos = blk * Q_BLOCK + jnp.arange(Q_BLOCK)
        causal = k_pos[None, :] <= q_pos[:, None]
        probs = jax.nn.softmax(jnp.where(causal, scores, -jnp.inf), axis=-1)
        weights = probs[:, :, 0] - lam * probs[:, :, 1]
        return jnp.einsum('bhqk,bkhe->bqhe', weights.astype(v.dtype), v)

    out = lax.map(attend_block, (q_blocks, jnp.arange(n_blk)))
    out = jnp.moveaxis(out, 0, 1).reshape(b, s, ATT_HEADS, 2 * ATT_HEAD_DIM)
    out = rms_norm(out, subln_w) * (1.0 - lambda_init)
    return out.reshape(b, s, ATT_WIDTH)


def causal_depthwise_conv(u, w, bias):
    s = u.shape[1]
    up = jnp.pad(u, ((0, 0), (CONV_K - 1, 0), (0, 0)))
    out = bias
    for tap in range(CONV_K):
        out = out + up[:, tap:tap + s] * w[tap]
    return out


def ssd_chunked_scan(xs, dt, a, bm, cm):
    b, s = xs.shape[:2]
    nc = s // SSM_CHUNK

    def chunks(t):
        return jnp.moveaxis(t.reshape((b, nc, SSM_CHUNK) + t.shape[2:]), 1, 0)

    causal = jnp.tril(jnp.ones((SSM_CHUNK, SSM_CHUNK), dtype=bool))[None, :, :, None, None]

    def step(state, inp):
        xc, dtc, bc, cc = inp
        acum = jnp.cumsum(dtc * a, axis=1)
        seg = acum[:, :, None] - acum[:, None, :]
        decay = jnp.exp(jnp.where(causal, seg, -jnp.inf))
        cb = jnp.einsum('blgn,bsgn->blsg', cc, bc)
        w = cb[..., None] * decay * dtc[:, None]
        y = jnp.einsum('blsgh,bsghp->blghp', w, xc)
        y = y + jnp.einsum('blgn,bghpn->blghp', cc, state) * jnp.exp(acum)[..., None]
        to_end = jnp.exp(acum[:, -1:] - acum) * dtc
        state = (state * jnp.exp(acum[:, -1])[..., None, None]
                 + jnp.einsum('blgn,blgh,blghp->bghpn', bc, to_end, xc))
        return state, y

    init = jnp.zeros((b, SSM_GROUPS, SSM_HEADS_PER_GROUP, SSM_HEAD_DIM, SSM_STATE), jnp.float32)
    _, y = lax.scan(step, init, (chunks(xs), chunks(dt), chunks(bm), chunks(cm)))
    return jnp.moveaxis(y, 0, 1).reshape(xs.shape)


def ssd_mixer(z, xbc, dt, conv_w, conv_b, dt_bias, a_log, d_skip, norm_w):
    b, s, _ = xbc.shape
    f32 = jnp.float32
    xbc = jax.nn.silu(causal_depthwise_conv(xbc, conv_w, conv_b))
    xs, bm, cm = jnp.split(xbc, [SSM_WIDTH, SSM_WIDTH + SSM_GROUPS * SSM_STATE], axis=-1)
    xs = xs.reshape(b, s, SSM_GROUPS, SSM_HEADS_PER_GROUP, SSM_HEAD_DIM).astype(f32)
    bm = bm.reshape(b, s, SSM_GROUPS, SSM_STATE).astype(f32)
    cm = cm.reshape(b, s, SSM_GROUPS, SSM_STATE).astype(f32)
    dt = jax.nn.softplus(dt.astype(f32) + dt_bias.astype(f32))
    dt = dt.reshape(b, s, SSM_GROUPS, SSM_HEADS_PER_GROUP)
    a = -jnp.exp(a_log.astype(f32)).reshape(SSM_GROUPS, SSM_HEADS_PER_GROUP)
    y = ssd_chunked_scan(xs, dt, a, bm, cm)
    y = y + d_skip.astype(f32).reshape(SSM_GROUPS, SSM_HEADS_PER_GROUP, 1) * xs
    y = y.reshape(b, s, SSM_WIDTH).astype(z.dtype)
    gated = (y * jax.nn.silu(z)).reshape(b, s, SSM_GROUPS, SSM_WIDTH // SSM_GROUPS)
    gated = rms_norm(gated, norm_w.reshape(SSM_GROUPS, SSM_WIDTH // SSM_GROUPS))
    return gated.reshape(b, s, SSM_WIDTH)


def moe_swiglu(h, router_w, w_gate, w_up, w_down):
    b, s, d = h.shape
    xt = h.reshape(b * s, d)
    logits = (xt @ router_w).astype(jnp.float32)
    top_vals, top_idx = lax.top_k(logits, TOP_K)
    top_gates = jax.nn.softmax(top_vals, axis=-1)
    combine = jnp.sum(jax.nn.one_hot(top_idx, N_EXPERTS, dtype=jnp.float32)
                      * top_gates[..., None], axis=1)
    out = jnp.zeros_like(xt)
    for e in range(N_EXPERTS):
        out = out + combine[:, e:e + 1].astype(xt.dtype) * swiglu(xt, w_gate[e], w_up[e], w_down[e])
    return out.reshape(b, s, d)


def setup_inputs(seed: int = 0) -> dict:
    key = jax.random.key(seed)
    ks = jax.random.split(key, 28)
    f32 = jnp.float32

    def normal(k, shape, scale):
        return jax.random.normal(k, shape, f32) * scale

    def gain(k, shape):
        return 1.0 + 0.02 * jax.random.normal(k, shape, f32)

    dt0 = jnp.exp(jax.random.uniform(ks[14], (DEPTH, SSM_HEADS), f32,
                                     minval=math.log(1e-3), maxval=math.log(1e-1)))
    return {
        "x": normal(ks[0], (BATCH, SEQ, D_MODEL), 1.0),
        "norm1_w": gain(ks[1], (DEPTH, D_MODEL)),
        "w_in": normal(ks[2], (DEPTH, D_MODEL, IN_COLS), D_MODEL ** -0.5),
        "w_out": normal(ks[3], (DEPTH, D_MIX, D_MODEL), D_MIX ** -0.5),
        "q_norm_w": gain(ks[4], (DEPTH, ATT_HEAD_DIM)),
        "k_norm_w": gain(ks[5], (DEPTH, ATT_HEAD_DIM)),
        "lambda_q1": normal(ks[6], (DEPTH, ATT_HEAD_DIM), 0.1),
        "lambda_k1": normal(ks[7], (DEPTH, ATT_HEAD_DIM), 0.1),
        "lambda_q2": normal(ks[8], (DEPTH, ATT_HEAD_DIM), 0.1),
        "lambda_k2": normal(ks[9], (DEPTH, ATT_HEAD_DIM), 0.1),
        "subln_w": gain(ks[10], (DEPTH, 2 * ATT_HEAD_DIM)),
        "conv_w": normal(ks[11], (DEPTH, CONV_K, CONV_CH), CONV_K ** -0.5),
        "conv_b": normal(ks[12], (DEPTH, CONV_CH), 0.02),
        "dt_bias": dt0 + jnp.log(-jnp.expm1(-dt0)),
        "a_log": jnp.log(jax.random.uniform(ks[15], (DEPTH, SSM_HEADS), f32, minval=1.0, maxval=16.0)),
        "d_skip": gain(ks[16], (DEPTH, SSM_HEADS)),
        "ssm_norm_w": gain(ks[17], (DEPTH, SSM_WIDTH)),
        "norm2_w": gain(ks[18], (DEPTH, D_MODEL)),
        "ffn_w_gate": normal(ks[19], (N_DENSE, D_MODEL, D_FF), D_MODEL ** -0.5),
        "ffn_w_up": normal(ks[20], (N_DENSE, D_MODEL, D_FF), D_MODEL ** -0.5),
        "ffn_w_down": normal(ks[21], (N_DENSE, D_FF, D_MODEL), D_FF ** -0.5),
        "router_w": normal(ks[22], (N_MOE, D_MODEL, N_EXPERTS), D_MODEL ** -0.5),
        "moe_w_gate": normal(ks[23], (N_MOE, N_EXPERTS, D_MODEL, MOE_D_FF), D_MODEL ** -0.5),
        "moe_w_up": normal(ks[24], (N_MOE, N_EXPERTS, D_MODEL, MOE_D_FF), D_MODEL ** -0.5),
        "moe_w_down": normal(ks[25], (N_MOE, N_EXPERTS, MOE_D_FF, D_MODEL), MOE_D_FF ** -0.5),
    }


def reference(x, norm1_w, w_in, w_out, q_norm_w, k_norm_w, lambda_q1, lambda_k1, lambda_q2,
              lambda_k2, subln_w, conv_w, conv_b, dt_bias, a_log, d_skip, ssm_norm_w, norm2_w,
              ffn_w_gate, ffn_w_up, ffn_w_down, router_w, moe_w_gate, moe_w_up, moe_w_down):
    b, s, _ = x.shape
    splits = [ATT_WIDTH, 2 * ATT_WIDTH, 3 * ATT_WIDTH, 3 * ATT_WIDTH + SSM_WIDTH,
              3 * ATT_WIDTH + SSM_WIDTH + CONV_CH]
    for layer in range(DEPTH):
        h = rms_norm(x, norm1_w[layer])
        proj = h @ w_in[layer]
        q, k, v, z, xbc, dt = jnp.split(proj, splits, axis=-1)
        lam_init = lambda_init_for(layer)
        lam = (jnp.exp(jnp.sum(lambda_q1[layer] * lambda_k1[layer]).astype(jnp.float32))
               - jnp.exp(jnp.sum(lambda_q2[layer] * lambda_k2[layer]).astype(jnp.float32))
               + lam_init)
        att = diff_attention(q.reshape(b, s, ATT_HEADS, 2, ATT_HEAD_DIM),
                             k.reshape(b, s, ATT_HEADS, 2, ATT_HEAD_DIM),
                             v.reshape(b, s, ATT_HEADS, 2 * ATT_HEAD_DIM),
                             q_norm_w[layer], k_norm_w[layer], lam, lam_init, subln_w[layer])
        ssm = ssd_mixer(z, xbc, dt, conv_w[layer], conv_b[layer], dt_bias[layer], a_log[layer],
                        d_skip[layer], ssm_norm_w[layer])
        x = x + jnp.concatenate([att, ssm], axis=-1) @ w_out[layer]
        h = rms_norm(x, norm2_w[layer])
        if layer % 2 == 0:
            i = layer // 2
            x = x + swiglu(h, ffn_w_gate[i], ffn_w_up[i], ffn_w_down[i])
        else:
            i = layer // 2
            x = x + moe_swiglu(h, router_w[i], moe_w_gate[i], moe_w_up[i], moe_w_down[i])
    return x
```

```python
import functools
import math

import jax
import jax.numpy as jnp
from jax import lax
from jax.experimental import pallas as pl
from jax.experimental.pallas import tpu as pltpu

F32 = jnp.float32
BF16 = jnp.bfloat16
EPS = 1e-6
LANES = 128
VMEM_LIMIT = 56 * 1024 * 1024

ATT_HEAD_DIM = 128
SSM_HEAD_DIM = 64
SSM_GROUPS = 8
SSM_STATE = 128
CONV_K = 4
SSM_CHUNK = 128
TOP_K = 2
MOE_TILE = 512


def _params(*sem):
    return pltpu.CompilerParams(dimension_semantics=sem, vmem_limit_bytes=VMEM_LIMIT)


def _dot(a, b):
    return jnp.dot(a, b, preferred_element_type=F32)


def _dot_nt(a, b):
    return lax.dot_general(a, b, (((1,), (1,)), ((), ())), preferred_element_type=F32)


def _silu(x):
    return x / (1.0 + jnp.exp(-x))


def _split3(x):
    hi = x.astype(BF16)
    r1 = x - hi.astype(F32)
    mid = r1.astype(BF16)
    lo = (r1 - mid.astype(F32)).astype(BF16)
    return hi, mid, lo


def _rmsnorm_kernel(x_ref, w_ref, o_ref):
    x = x_ref[...]
    ms = jnp.mean(x * x, axis=-1, keepdims=True)
    o_ref[...] = (x * lax.rsqrt(ms + EPS) * w_ref[...]).astype(o_ref.dtype)


def rmsnorm(x, w, out_dtype=BF16, bt=256):
    t, d = x.shape
    bt = min(bt, t)
    return pl.pallas_call(
        _rmsnorm_kernel,
        grid=(t // bt,),
        in_specs=[pl.BlockSpec((bt, d), lambda i: (i, 0)),
                  pl.BlockSpec((1, d), lambda i: (0, 0))],
        out_specs=pl.BlockSpec((bt, d), lambda i: (i, 0)),
        out_shape=jax.ShapeDtypeStruct((t, d), out_dtype),
        compiler_params=_params("parallel"),
        name="rmsnorm",
    )(x, w.reshape(1, d))


def _mm_kernel(x_ref, w_ref, o_ref):
    o_ref[...] = _dot(x_ref[...], w_ref[...]).astype(o_ref.dtype)


def _mm_headnorm_kernel(x_ref, w_ref, g_ref, o_ref):
    acc = _dot(x_ref[...], w_ref[...])
    for c in range(acc.shape[1] // ATT_HEAD_DIM):
        sl = slice(c * ATT_HEAD_DIM, (c + 1) * ATT_HEAD_DIM)
        a = acc[:, sl]
        ms = jnp.mean(a * a, axis=-1, keepdims=True)
        o_ref[:, sl] = (a * lax.rsqrt(ms + EPS) * g_ref[:, sl]).astype(o_ref.dtype)


def _mm_res_kernel(x_ref, w_ref, r_ref, o_ref):
    o_ref[...] = r_ref[...] + _dot(x_ref[...], w_ref[...])


def _mm_res_ktiled_kernel(x_ref, w_ref, r_ref, o_ref):
    k = pl.program_id(2)

    @pl.when(k == 0)
    def _():
        o_ref[...] = r_ref[...] + _dot(x_ref[...], w_ref[...])

    @pl.when(k != 0)
    def _():
        o_ref[...] += _dot(x_ref[...], w_ref[...])


def _swiglu_up_kernel(x_ref, wg_ref, wu_ref, o_ref):
    x = x_ref[...]
    g = _dot(x, wg_ref[...])
    u = _dot(x, wu_ref[...])
    o_ref[...] = (_silu(g) * u).astype(o_ref.dtype)


def _mm_tiles(m, n, bm, bn):
    bm = min(bm, m)
    bn = min(bn, n)
    assert m % bm == 0 and n % bn == 0, (m, n, bm, bn)
    return bm, bn


def matmul(x, w, out_dtype, bm=1024, bn=512, name="matmul"):
    m, k = x.shape
    n = w.shape[1]
    bm, bn = _mm_tiles(m, n, bm, bn)
    return pl.pallas_call(
        _mm_kernel,
        grid=(m // bm, n // bn),
        in_specs=[pl.BlockSpec((bm, k), lambda i, j: (i, 0)),
                  pl.BlockSpec((k, bn), lambda i, j: (0, j))],
        out_specs=pl.BlockSpec((bm, bn), lambda i, j: (i, j)),
        out_shape=jax.ShapeDtypeStruct((m, n), out_dtype),
        compiler_params=_params("parallel", "arbitrary"),
        name=name,
    )(x, w)


def matmul_headnorm(x, w, gain, bm=1024, bn=512):
    m, k = x.shape
    n = w.shape[1]
    bm, bn = _mm_tiles(m, n, bm, bn)
    return pl.pallas_call(
        _mm_headnorm_kernel,
        grid=(m // bm, n // bn),
        in_specs=[pl.BlockSpec((bm, k), lambda i, j: (i, 0)),
                  pl.BlockSpec((k, bn), lambda i, j: (0, j)),
                  pl.BlockSpec((1, bn), lambda i, j: (0, j))],
        out_specs=pl.BlockSpec((bm, bn), lambda i, j: (i, j)),
        out_shape=jax.ShapeDtypeStruct((m, n), BF16),
        compiler_params=_params("parallel", "arbitrary"),
        name="matmul_headnorm",
    )(x, w, gain)


def matmul_residual(x, w, res, bm=1024, bn=512, bk=None, name="matmul_residual"):
    m, k = x.shape
    n = w.shape[1]
    bm, bn = _mm_tiles(m, n, bm, bn)
    if bk is None or bk >= k:
        return pl.pallas_call(
            _mm_res_kernel,
            grid=(m // bm, n // bn),
            in_specs=[pl.BlockSpec((bm, k), lambda i, j: (i, 0)),
                      pl.BlockSpec((k, bn), lambda i, j: (0, j)),
                      pl.BlockSpec((bm, bn), lambda i, j: (i, j))],
            out_specs=pl.BlockSpec((bm, bn), lambda i, j: (i, j)),
            out_shape=jax.ShapeDtypeStruct((m, n), F32),
            compiler_params=_params("parallel", "arbitrary"),
            name=name,
        )(x, w, res)
    assert k % bk == 0
    return pl.pallas_call(
        _mm_res_ktiled_kernel,
        grid=(m // bm, n // bn, k // bk),
        in_specs=[pl.BlockSpec((bm, bk), lambda i, j, kk: (i, kk)),
                  pl.BlockSpec((bk, bn), lambda i, j, kk: (kk, j)),
                  pl.BlockSpec((bm, bn), lambda i, j, kk: (i, j))],
        out_specs=pl.BlockSpec((bm, bn), lambda i, j, kk: (i, j)),
        out_shape=jax.ShapeDtypeStruct((m, n), F32),
        compiler_params=_params("parallel", "arbitrary", "arbitrary"),
        name=name,
    )(x, w, res)


def swiglu_up(x, wg, wu, bm=1024, bn=512):
    m, k = x.shape
    n = wg.shape[1]
    bm, bn = _mm_tiles(m, n, bm, bn)
    return pl.pallas_call(
        _swiglu_up_kernel,
        grid=(m // bm, n // bn),
        in_specs=[pl.BlockSpec((bm, k), lambda i, j: (i, 0)),
                  pl.BlockSpec((k, bn), lambda i, j: (0, j)),
                  pl.BlockSpec((k, bn), lambda i, j: (0, j))],
        out_specs=pl.BlockSpec((bm, bn), lambda i, j: (i, j)),
        out_shape=jax.ShapeDtypeStruct((m, n), BF16),
        compiler_params=_params("parallel", "arbitrary"),
        name="swiglu_up",
    )(x, wg, wu)


def _attn_kernel(lam_ref, q_ref, k_ref, v_ref, w_ref, o_ref, m_sc, l_sc, acc_sc, *, blk, lam_init):
    d = ATT_HEAD_DIM
    qi = pl.program_id(2)
    q = q_ref[...]

    m_sc[...] = jnp.full(m_sc.shape, -jnp.inf, F32)
    l_sc[...] = jnp.zeros(l_sc.shape, F32)
    acc_sc[...] = jnp.zeros(acc_sc.shape, F32)

    def step(kb, masked):
        start = pl.multiple_of(kb * blk, blk)
        k = k_ref[pl.ds(start, blk), :]
        v = v_ref[pl.ds(start, blk), :]
        if masked:
            row = lax.broadcasted_iota(jnp.int32, (blk, blk), 0)
            col = lax.broadcasted_iota(jnp.int32, (blk, blk), 1)
            keep = col <= row
        for c in range(2):
            s = _dot_nt(q[:, c * d:(c + 1) * d], k[:, c * d:(c + 1) * d])
            if masked:
                s = jnp.where(keep, s, -jnp.inf)
            m_prev = m_sc[c][:, :1]
            m_new = jnp.maximum(m_prev, jnp.max(s, axis=-1, keepdims=True))
            alpha = jnp.exp2(m_prev - m_new)
            p = jnp.exp2(s - m_new)
            l_sc[c] = jnp.broadcast_to(alpha * l_sc[c][:, :1] + jnp.sum(p, axis=-1, keepdims=True),
                                       (blk, LANES))
            m_sc[c] = jnp.broadcast_to(m_new, (blk, LANES))
            acc_sc[c] = alpha * acc_sc[c] + _dot(p.astype(BF16), v)

    def body(kb, carry):
        step(kb, False)
        return carry

    lax.fori_loop(0, qi, body, 0)
    step(qi, True)

    lv = lam_ref[...]
    a1 = jnp.sum(lv[0:1] * lv[1:2], axis=-1, keepdims=True)
    a2 = jnp.sum(lv[2:3] * lv[3:4], axis=-1, keepdims=True)
    lam = jnp.exp(a1) - jnp.exp(a2) + lam_init
    o = acc_sc[0] / l_sc[0][:, :1] - lam * (acc_sc[1] / l_sc[1][:, :1])
    ms = jnp.mean(o * o, axis=-1, keepdims=True)
    o_ref[...] = (o * lax.rsqrt(ms + EPS) * (w_ref[...] * (1.0 - lam_init))).astype(o_ref.dtype)


def diff_attention(qk, v, lam_vecs, subln_w, lam_init, batch, seq, blk=512):
    t, width = v.shape
    hv = 2 * ATT_HEAD_DIM
    heads = width // hv
    blk = min(blk, seq)
    nq = seq // blk
    kern = functools.partial(_attn_kernel, blk=blk, lam_init=lam_init)
    return pl.pallas_call(
        kern,
        grid=(batch, heads, nq),
        in_specs=[pl.BlockSpec((4, ATT_HEAD_DIM), lambda b, h, i: (0, 0)),
                  pl.BlockSpec((blk, hv), lambda b, h, i: (b * nq + i, h)),
                  pl.BlockSpec((seq, hv), lambda b, h, i: (b, heads + h)),
                  pl.BlockSpec((seq, hv), lambda b, h, i: (b, h)),
                  pl.BlockSpec((1, hv), lambda b, h, i: (0, 0))],
        out_specs=pl.BlockSpec((blk, hv), lambda b, h, i: (b * nq + i, h)),
        out_shape=jax.ShapeDtypeStruct((t, width), BF16),
        scratch_shapes=[pltpu.VMEM((2, blk, LANES), F32),
                        pltpu.VMEM((2, blk, LANES), F32),
                        pltpu.VMEM((2, blk, hv), F32)],
        compiler_params=_params("parallel", "parallel", "arbitrary"),
        name="diff_attention",
    )(lam_vecs, qk, qk, v, subln_w.reshape(1, hv))


def _ssd_kernel(xbc_ref, z_ref, dt_ref, cw_ref, cb_ref, dtb_ref, alog_ref, dsk_ref, nw_ref,
                o_ref, ext_sc, state_sc, *, width):
    L = SSM_CHUNK
    n = SSM_STATE
    gw = width // SSM_GROUPS
    hpg = gw // SSM_HEAD_DIM
    pad = 8
    c = pl.program_id(1)

    @pl.when(c == 0)
    def _():
        ext_sc[0:pad, :] = jnp.zeros((pad, ext_sc.shape[1]), F32)
        state_sc[...] = jnp.zeros(state_sc.shape, F32)

    ext_sc[pad:pad + L, :] = xbc_ref[...]

    def conv_silu(lo, hi):
        acc = cb_ref[:, lo:hi]
        for tap in range(CONV_K):
            acc = acc + ext_sc[pl.ds(pad - (CONV_K - 1) + tap, L), lo:hi] * cw_ref[tap:tap + 1, lo:hi]
        return _silu(acc)

    dtr = dt_ref[...] + dtb_ref[...]
    dtp = jnp.maximum(dtr, 0.0) + jnp.log1p(jnp.exp(-jnp.abs(dtr)))
    a = -jnp.exp(alog_ref[...])
    d_a = dtp * a
    row = lax.broadcasted_iota(jnp.int32, (L, L), 0)
    col = lax.broadcasted_iota(jnp.int32, (L, L), 1)
    causal = col <= row
    tril = jnp.where(causal, 1.0, 0.0).astype(BF16)
    hi_, mid_, lo_ = _split3(d_a)
    acum = _dot(tril, hi_) + _dot(tril, mid_) + _dot(tril, lo_)
    acum_t = acum.T
    dtp_t = dtp.T
    e_acum = jnp.exp(acum)
    a_last = acum[L - 1:L, :]
    to_end = jnp.exp(a_last - acum) * dtp
    e_last = jnp.exp(a_last)

    head_of_col = lax.broadcasted_iota(jnp.int32, (1, gw), 1) // SSM_HEAD_DIM

    def expand(x, g):
        out = x[:, g * hpg:g * hpg + 1]
        for hd in range(1, hpg):
            out = jnp.where(head_of_col == hd, x[:, g * hpg + hd:g * hpg + hd + 1], out)
        return jnp.broadcast_to(out, (x.shape[0], gw))

    for g in range(SSM_GROUPS):
        xg = conv_silu(g * gw, (g + 1) * gw)
        bg = conv_silu(width + g * n, width + (g + 1) * n)
        cg = conv_silu(width + SSM_GROUPS * n + g * n, width + SSM_GROUPS * n + (g + 1) * n)
        xg16 = xg.astype(BF16)
        cg16 = cg.astype(BF16)
        cb = _dot_nt(cg16, bg.astype(BF16))
        y = jnp.zeros((L, gw), F32)
        for hd in range(hpg):
            h = g * hpg + hd
            seg = acum[:, h:h + 1] - acum_t[h:h + 1, :]
            decay = jnp.exp(jnp.where(causal, seg, -jnp.inf))
            w = cb * decay * dtp_t[h:h + 1, :]
            xh = jnp.where(head_of_col == hd, xg16, jnp.zeros_like(xg16))
            y = y + _dot(w.astype(BF16), xh)
        st = state_sc[g]
        y = y + _dot(cg16, st.astype(BF16)) * expand(e_acum, g)
        xw = (xg * expand(to_end, g)).astype(BF16)
        state_sc[g] = st * expand(e_last, g) + _dot(bg.T.astype(BF16), xw)
        y = y + dsk_ref[:, g * gw:(g + 1) * gw] * xg
        gated = y * _silu(z_ref[:, g * gw:(g + 1) * gw])
        ms = jnp.mean(gated * gated, axis=-1, keepdims=True)
        o_ref[:, g * gw:(g + 1) * gw] = (gated * lax.rsqrt(ms + EPS)
                                         * nw_ref[:, g * gw:(g + 1) * gw]).astype(o_ref.dtype)

    ext_sc[0:pad, :] = xbc_ref[L - pad:L, :]


def ssd_mixer(cz, dt, conv_w, conv_b, dt_bias, a_log, d_skip, norm_w, batch, seq):
    t = cz.shape[0]
    conv_ch = conv_w.shape[1]
    width = norm_w.shape[0]
    heads = width // SSM_HEAD_DIM
    L = SSM_CHUNK
    nc = seq // L
    assert cz.shape[1] == conv_ch + width and conv_ch % width == 0
    z_blk = conv_ch // width
    pad_heads = lambda v: jnp.pad(v.reshape(1, heads), ((0, 0), (0, LANES - heads)))
    dsk = jnp.repeat(d_skip, SSM_HEAD_DIM).reshape(1, width)
    return pl.pallas_call(
        functools.partial(_ssd_kernel, width=width),
        grid=(batch, nc),
        in_specs=[pl.BlockSpec((L, conv_ch), lambda b, c: (b * nc + c, 0)),
                  pl.BlockSpec((L, width), lambda b, c: (b * nc + c, z_blk)),
                  pl.BlockSpec((L, LANES), lambda b, c: (b * nc + c, 0)),
                  pl.BlockSpec((CONV_K, conv_ch), lambda b, c: (0, 0)),
                  pl.BlockSpec((1, conv_ch), lambda b, c: (0, 0)),
                  pl.BlockSpec((1, LANES), lambda b, c: (0, 0)),
                  pl.BlockSpec((1, LANES), lambda b, c: (0, 0)),
                  pl.BlockSpec((1, width), lambda b, c: (0, 0)),
                  pl.BlockSpec((1, width), lambda b, c: (0, 0))],
        out_specs=pl.BlockSpec((L, width), lambda b, c: (b * nc + c, 0)),
        out_shape=jax.ShapeDtypeStruct((t, width), BF16),
        scratch_shapes=[pltpu.VMEM((8 + L, conv_ch), F32),
                        pltpu.VMEM((SSM_GROUPS, SSM_STATE, width // SSM_GROUPS), F32)],
        compiler_params=_params("parallel", "arbitrary"),
        name="ssd_mixer",
    )(cz, cz, dt, conv_w, conv_b.reshape(1, conv_ch), pad_heads(dt_bias), pad_heads(a_log), dsk,
      norm_w.reshape(1, width))


def _norm_router_kernel(x_ref, w_ref, rw_ref, h_ref, idx_ref, gate_ref, *, n_experts):
    x = x_ref[...]
    ms = jnp.mean(x * x, axis=-1, keepdims=True)
    hn = x * lax.rsqrt(ms + EPS) * w_ref[...]
    h_ref[...] = hn
    hh, hm, _ = _split3(hn)
    rh, rm, _ = _split3(rw_ref[...])
    logits = _dot(hh, rh) + _dot(hh, rm) + _dot(hm, rh)
    lane = lax.broadcasted_iota(jnp.int32, logits.shape, 1)
    lanef = lane.astype(F32)
    logits = jnp.where(lane < n_experts, logits, -jnp.inf)
    m1 = jnp.max(logits, axis=-1, keepdims=True)
    i1 = jnp.min(jnp.where(logits == m1, lanef, float(LANES)), axis=-1, keepdims=True)
    rest = jnp.where(lanef == i1, -jnp.inf, logits)
    m2 = jnp.max(rest, axis=-1, keepdims=True)
    i2 = jnp.min(jnp.where(rest == m2, lanef, float(LANES)), axis=-1, keepdims=True)
    e = jnp.exp(m2 - m1)
    g1 = 1.0 / (1.0 + e)
    idx_ref[...] = jnp.where(lane == 0, i1, i2).astype(jnp.int32)
    gate_ref[...] = jnp.where(lane == 0, g1, e * g1)


def norm_router(x, w, router_w, bt=256):
    t, d = x.shape
    n_experts = router_w.shape[1]
    bt = min(bt, t)
    rw = jnp.pad(router_w, ((0, 0), (0, LANES - n_experts)))
    return pl.pallas_call(
        functools.partial(_norm_router_kernel, n_experts=n_experts),
        grid=(t // bt,),
        in_specs=[pl.BlockSpec((bt, d), lambda i: (i, 0)),
                  pl.BlockSpec((1, d), lambda i: (0, 0)),
                  pl.BlockSpec((d, LANES), lambda i: (0, 0))],
        out_specs=[pl.BlockSpec((bt, d), lambda i: (i, 0)),
                   pl.BlockSpec((bt, LANES), lambda i: (i, 0)),
                   pl.BlockSpec((bt, LANES), lambda i: (i, 0))],
        out_shape=[jax.ShapeDtypeStruct((t, d), F32),
                   jax.ShapeDtypeStruct((t, LANES), jnp.int32),
                   jax.ShapeDtypeStruct((t, LANES), F32)],
        compiler_params=_params("parallel"),
        name="norm_router",
    )(x, w.reshape(1, d), rw)


def _gather_rows(idx_ref, n_rows, src_hbm, dst_ref, sem):
    def issue(r, carry):
        pltpu.make_async_copy(src_hbm.at[pl.ds(idx_ref[0, 0, r], 1), :],
                              dst_ref.at[pl.ds(r, 1), :], sem).start()
        return carry

    lax.fori_loop(0, n_rows, issue, 0)
    pltpu.make_async_copy(src_hbm.at[pl.ds(0, n_rows), :], dst_ref, sem).wait()


def _moe_up_kernel(te_ref, tv_ref, tok_ref, h_hbm, wg_ref, wu_ref, o_ref, xf_sc, xb_sc, sem):
    i = pl.program_id(0)
    j = pl.program_id(1)
    valid = tv_ref[i] != 0

    @pl.when(jnp.logical_and(valid, j == 0))
    def _():
        _gather_rows(tok_ref, xf_sc.shape[0], h_hbm, xf_sc, sem)
        xb_sc[...] = xf_sc[...].astype(BF16)

    @pl.when(valid)
    def _():
        x = xb_sc[...]
        g = _dot(x, wg_ref[...])
        u = _dot(x, wu_ref[...])
        o_ref[...] = (_silu(g) * u).astype(o_ref.dtype)

    @pl.when(jnp.logical_not(valid))
    def _():
        o_ref[...] = jnp.zeros(o_ref.shape, o_ref.dtype)


def _moe_down_kernel(te_ref, tv_ref, h_ref, wd_ref, g_ref, o_ref):
    i = pl.program_id(0)
    valid = tv_ref[i] != 0

    @pl.when(valid)
    def _():
        o_ref[...] = _dot(h_ref[...], wd_ref[...]) * g_ref[...]

    @pl.when(jnp.logical_not(valid))
    def _():
        o_ref[...] = jnp.zeros(o_ref.shape, o_ref.dtype)


def _moe_combine_kernel(pos_ref, x_ref, y_hbm, o_ref, rows_sc, sem):
    bt = x_ref.shape[0]
    _gather_rows(pos_ref, TOP_K * bt, y_hbm, rows_sc, sem)
    o_ref[...] = x_ref[...] + (rows_sc[0:bt, :] + rows_sc[bt:2 * bt, :])


def moe_ffn(xt, norm_w, router_w, w_gate, w_up, w_down, bn=512, bt=256):
    t, d = xt.shape
    n_experts, _, f = w_gate.shape
    bm = min(MOE_TILE, t)
    bt = min(bt, t)
    h, idx, gates = norm_router(xt, norm_w, router_w)

    e_flat = idx[:, :TOP_K].reshape(-1)
    g_flat = gates[:, :TOP_K].reshape(-1)
    onehot = (e_flat[:, None] == jnp.arange(n_experts, dtype=jnp.int32)[None, :]).astype(jnp.int32)
    csum = jnp.cumsum(onehot, axis=0)
    rank = jnp.sum((csum - onehot) * onehot, axis=1)
    counts = csum[-1]
    tiles_e = (counts + bm - 1) // bm
    tile_end = jnp.cumsum(tiles_e)
    row_start = (tile_end - tiles_e) * bm
    pos = row_start[e_flat] + rank
    n_tiles = (TOP_K * t) // bm + n_experts
    n_rows = n_tiles * bm
    row_token = jnp.zeros((n_rows,), jnp.int32).at[pos].set(jnp.arange(TOP_K * t, dtype=jnp.int32) // TOP_K)
    row_gate = jnp.zeros((n_rows,), F32).at[pos].set(g_flat)
    tile_ids = jnp.arange(n_tiles, dtype=jnp.int32)
    n_valid = tile_end[-1]
    tile_valid = (tile_ids < n_valid).astype(jnp.int32)
    tile_expert = jnp.searchsorted(tile_end, jnp.minimum(tile_ids, n_valid - 1), side="right").astype(jnp.int32)
    tile_expert = jnp.minimum(tile_expert, n_experts - 1)

    nj = f // bn
    wcol = lambda i, j, te, tv, last: jnp.where(tv[i] != 0, j, last)
    hid = pl.pallas_call(
        _moe_up_kernel,
        grid_spec=pltpu.PrefetchScalarGridSpec(
            num_scalar_prefetch=2,
            grid=(n_tiles, nj),
            in_specs=[pl.BlockSpec((1, 1, bm), lambda i, j, te, tv: (i, 0, 0), memory_space=pltpu.SMEM),
                      pl.BlockSpec(memory_space=pl.ANY),
                      pl.BlockSpec((None, d, bn), lambda i, j, te, tv: (te[i], 0, wcol(i, j, te, tv, nj - 1))),
                      pl.BlockSpec((None, d, bn), lambda i, j, te, tv: (te[i], 0, wcol(i, j, te, tv, nj - 1)))],
            out_specs=pl.BlockSpec((bm, bn), lambda i, j, te, tv: (i, j)),
            scratch_shapes=[pltpu.VMEM((bm, d), F32), pltpu.VMEM((bm, d), BF16),
                            pltpu.SemaphoreType.DMA(())]),
        out_shape=jax.ShapeDtypeStruct((n_rows, f), BF16),
        compiler_params=_params("arbitrary", "arbitrary"),
        name="moe_up",
    )(tile_expert, tile_valid, row_token.reshape(n_tiles, 1, bm), h, w_gate, w_up)

    nd = d // bn
    y = pl.pallas_call(
        _moe_down_kernel,
        grid_spec=pltpu.PrefetchScalarGridSpec(
            num_scalar_prefetch=2,
            grid=(n_tiles, nd),
            in_specs=[pl.BlockSpec((bm, f), lambda i, j, te, tv: (i, 0)),
                      pl.BlockSpec((None, f, bn), lambda i, j, te, tv: (te[i], 0, wcol(i, j, te, tv, nd - 1))),
                      pl.BlockSpec((bm, 1), lambda i, j, te, tv: (i, 0))],
            out_specs=pl.BlockSpec((bm, bn), lambda i, j, te, tv: (i, j))),
        out_shape=jax.ShapeDtypeStruct((n_rows, d), F32),
        compiler_params=_params("arbitrary", "arbitrary"),
        name="moe_down",
    )(tile_expert, tile_valid, hid, w_down, row_gate.reshape(n_rows, 1))

    pos_tiles = pos.reshape(t // bt, bt, TOP_K).transpose(0, 2, 1).reshape(t // bt, 1, TOP_K * bt)
    return pl.pallas_call(
        _moe_combine_kernel,
        grid=(t // bt,),
        in_specs=[pl.BlockSpec((1, 1, TOP_K * bt), lambda i: (i, 0, 0), memory_space=pltpu.SMEM),
                  pl.BlockSpec((bt, d), lambda i: (i, 0)),
                  pl.BlockSpec(memory_space=pl.ANY)],
        out_specs=pl.BlockSpec((bt, d), lambda i: (i, 0)),
        out_shape=jax.ShapeDtypeStruct((t, d), F32),
        scratch_shapes=[pltpu.VMEM((TOP_K * bt, d), F32), pltpu.SemaphoreType.DMA(())],
        compiler_params=_params("arbitrary"),
        name="moe_combine",
    )(pos_tiles, xt, y)


def _lambda_init(layer):
    return 0.8 - 0.6 * math.exp(-0.3 * layer)


def _round_up(n, m):
    return (n + m - 1) // m * m


def kernel(x, norm1_w, w_in, w_out, q_norm_w, k_norm_w, lambda_q1, lambda_k1, lambda_q2, lambda_k2,
           subln_w, conv_w, conv_b, dt_bias, a_log, d_skip, ssm_norm_w, norm2_w, ffn_w_gate, ffn_w_up,
           ffn_w_down, router_w, moe_w_gate, moe_w_up, moe_w_down):
    b, s, d = x.shape
    t = b * s
    depth = w_in.shape[0]
    ssm_width = ssm_norm_w.shape[1]
    conv_ch = conv_w.shape[2]
    ssm_heads = dt_bias.shape[1]
    att_width = (w_in.shape[2] - ssm_width - conv_ch - ssm_heads) // 3
    n_qk_heads = att_width // ATT_HEAD_DIM
    q_scale = ATT_HEAD_DIM ** -0.5 * math.log2(math.e)

    xt = x.reshape(t, d)
    for layer in range(depth):
        wl = w_in[layer]
        z0 = 3 * att_width
        c0 = z0 + ssm_width
        w_qk = wl[:, :2 * att_width].astype(BF16)
        w_v = wl[:, 2 * att_width:z0].astype(BF16)
        w_cz = jnp.concatenate([wl[:, c0:c0 + conv_ch], wl[:, z0:c0]], axis=1).astype(BF16)
        w_dt = jnp.pad(wl[:, c0 + conv_ch:], ((0, 0), (0, LANES - ssm_heads))).astype(BF16)

        h = rmsnorm(xt, norm1_w[layer])
        gain = jnp.concatenate([jnp.tile(q_norm_w[layer] * q_scale, n_qk_heads),
                                jnp.tile(k_norm_w[layer], n_qk_heads)]).reshape(1, 2 * att_width)
        qk = matmul_headnorm(h, w_qk, gain)
        v = matmul(h, w_v, BF16, name="proj_v")
        cz = matmul(h, w_cz, F32, name="proj_ssm")
        dt = matmul(h, w_dt, F32, bn=LANES, name="proj_dt")

        lam_vecs = jnp.stack([lambda_q1[layer], lambda_k1[layer], lambda_q2[layer], lambda_k2[layer]])
        att = diff_attention(qk, v, lam_vecs, subln_w[layer], _lambda_init(layer), b, s)
        ssm = ssd_mixer(cz, dt, conv_w[layer], conv_b[layer], dt_bias[layer], a_log[layer],
                        d_skip[layer], ssm_norm_w[layer], b, s)
        mix = jnp.concatenate([att, ssm], axis=1)
        xt = matmul_residual(mix, w_out[layer].astype(BF16), xt, name="out_proj")

        i = layer // 2
        if layer % 2 == 0:
            f = ffn_w_gate.shape[2]
            fp = _round_up(f, 512)
            padc = lambda w: jnp.pad(w, ((0, 0), (0, fp - f))).astype(BF16)
            h2 = rmsnorm(xt, norm2_w[layer])
            hid = swiglu_up(h2, padc(ffn_w_gate[i]), padc(ffn_w_up[i]))
            wd = jnp.pad(ffn_w_down[i], ((0, fp - f), (0, 0))).astype(BF16)
            xt = matmul_residual(hid, wd, xt, bk=fp // 4, name="ffn_down")
        else:
            xt = moe_ffn(xt, norm2_w[layer], router_w[i], moe_w_gate[i].astype(BF16),
                         moe_w_up[i].astype(BF16), moe_w_down[i].astype(BF16))
    return xt.reshape(b, s, d)
```

```python
import functools
import math

import jax
import jax.numpy as jnp
from jax import lax
from jax.experimental import pallas as pl
from jax.experimental.pallas import tpu as pltpu

F32 = jnp.float32
BF16 = jnp.bfloat16
EPS = 1e-6
LANES = 128
VMEM_LIMIT = 56 * 1024 * 1024

ATT_HEAD_DIM = 128
SSM_HEAD_DIM = 64
SSM_GROUPS = 8
SSM_STATE = 128
CONV_K = 4
SSM_CHUNK = 128
TOP_K = 2
MOE_TILE = 512


def _params(*sem):
    return pltpu.CompilerParams(dimension_semantics=sem, vmem_limit_bytes=VMEM_LIMIT)


def _dot(a, b):
    return jnp.dot(a, b, preferred_element_type=F32)


def _dot_nt(a, b):
    return lax.dot_general(a, b, (((1,), (1,)), ((), ())), preferred_element_type=F32)


def _silu(x):
    return x / (1.0 + jnp.exp(-x))


def _split3(x):
    hi = x.astype(BF16)
    r1 = x - hi.astype(F32)
    mid = r1.astype(BF16)
    lo = (r1 - mid.astype(F32)).astype(BF16)
    return hi, mid, lo


def _rmsnorm_kernel(x_ref, w_ref, o_ref):
    x = x_ref[...]
    ms = jnp.mean(x * x, axis=-1, keepdims=True)
    o_ref[...] = (x * lax.rsqrt(ms + EPS) * w_ref[...]).astype(o_ref.dtype)


def rmsnorm(x, w, out_dtype=BF16, bt=256):
    t, d = x.shape
    bt = min(bt, t)
    return pl.pallas_call(
        _rmsnorm_kernel,
        grid=(t // bt,),
        in_specs=[pl.BlockSpec((bt, d), lambda i: (i, 0)),
                  pl.BlockSpec((1, d), lambda i: (0, 0))],
        out_specs=pl.BlockSpec((bt, d), lambda i: (i, 0)),
        out_shape=jax.ShapeDtypeStruct((t, d), out_dtype),
        compiler_params=_params("parallel"),
        name="rmsnorm",
    )(x, w.reshape(1, d))


def _mm_kernel(x_ref, w_ref, o_ref):
    o_ref[...] = _dot(x_ref[...], w_ref[...]).astype(o_ref.dtype)


def _mm_headnorm_kernel(x_ref, w_ref, g_ref, o_ref):
    acc = _dot(x_ref[...], w_ref[...])
    for c in range(acc.shape[1] // ATT_HEAD_DIM):
        sl = slice(c * ATT_HEAD_DIM, (c + 1) * ATT_HEAD_DIM)
        a = acc[:, sl]
        ms = jnp.mean(a * a, axis=-1, keepdims=True)
        o_ref[:, sl] = (a * lax.rsqrt(ms + EPS) * g_ref[:, sl]).astype(o_ref.dtype)


def _mm_res_kernel(x_ref, w_ref, r_ref, o_ref):
    o_ref[...] = r_ref[...] + _dot(x_ref[...], w_ref[...])


def _mm_res_ktiled_kernel(x_ref, w_ref, r_ref, o_ref):
    k = pl.program_id(2)

    @pl.when(k == 0)
    def _():
        o_ref[...] = r_ref[...] + _dot(x_ref[...], w_ref[...])

    @pl.when(k != 0)
    def _():
        o_ref[...] += _dot(x_ref[...], w_ref[...])


def _swiglu_up_kernel(x_ref, wg_ref, wu_ref, o_ref):
    x = x_ref[...]
    g = _dot(x, wg_ref[...])
    u = _dot(x, wu_ref[...])
    o_ref[...] = (_silu(g) * u).astype(o_ref.dtype)


def _mm_tiles(m, n, bm, bn):
    bm = min(bm, m)
    bn = min(bn, n)
    assert m % bm == 0 and n % bn == 0, (m, n, bm, bn)
    return bm, bn


def matmul(x, w, out_dtype, bm=1024, bn=512, name="matmul"):
    m, k = x.shape
    n = w.shape[1]
    bm, bn = _mm_tiles(m, n, bm, bn)
    return pl.pallas_call(
        _mm_kernel,
        grid=(m // bm, n // bn),
        in_specs=[pl.BlockSpec((bm, k), lambda i, j: (i, 0)),
                  pl.BlockSpec((k, bn), lambda i, j: (0, j))],
        out_specs=pl.BlockSpec((bm, bn), lambda i, j: (i, j)),
        out_shape=jax.ShapeDtypeStruct((m, n), out_dtype),
        compiler_params=_params("parallel", "arbitrary"),
        name=name,
    )(x, w)


def matmul_headnorm(x, w, gain, bm=1024, bn=512):
    m, k = x.shape
    n = w.shape[1]
    bm, bn = _mm_tiles(m, n, bm, bn)
    return pl.pallas_call(
        _mm_headnorm_kernel,
        grid=(m // bm, n // bn),
        in_specs=[pl.BlockSpec((bm, k), lambda i, j: (i, 0)),
                  pl.BlockSpec((k, bn), lambda i, j: (0, j)),
                  pl.BlockSpec((1, bn), lambda i, j: (0, j))],
        out_specs=pl.BlockSpec((bm, bn), lambda i, j: (i, j)),
        out_shape=jax.ShapeDtypeStruct((m, n), BF16),
        compiler_params=_params("parallel", "arbitrary"),
        name="matmul_headnorm",
    )(x, w, gain)


def matmul_residual(x, w, res, bm=1024, bn=512, bk=None, name="matmul_residual"):
    m, k = x.shape
    n = w.shape[1]
    bm, bn = _mm_tiles(m, n, bm, bn)
    if bk is None or bk >= k:
        return pl.pallas_call(
            _mm_res_kernel,
            grid=(m // bm, n // bn),
            in_specs=[pl.BlockSpec((bm, k), lambda i, j: (i, 0)),
                      pl.BlockSpec((k, bn), lambda i, j: (0, j)),
                      pl.BlockSpec((bm, bn), lambda i, j: (i, j))],
            out_specs=pl.BlockSpec((bm, bn), lambda i, j: (i, j)),
            out_shape=jax.ShapeDtypeStruct((m, n), F32),
            compiler_params=_params("parallel", "arbitrary"),
            name=name,
        )(x, w, res)
    assert k % bk == 0
    return pl.pallas_call(
        _mm_res_ktiled_kernel,
        grid=(m // bm, n // bn, k // bk),
        in_specs=[pl.BlockSpec((bm, bk), lambda i, j, kk: (i, kk)),
                  pl.BlockSpec((bk, bn), lambda i, j, kk: (kk, j)),
                  pl.BlockSpec((bm, bn), lambda i, j, kk: (i, j))],
        out_specs=pl.BlockSpec((bm, bn), lambda i, j, kk: (i, j)),
        out_shape=jax.ShapeDtypeStruct((m, n), F32),
        compiler_params=_params("parallel", "arbitrary", "arbitrary"),
        name=name,
    )(x, w, res)


def swiglu_up(x, wg, wu, bm=1024, bn=512):
    m, k = x.shape
    n = wg.shape[1]
    bm, bn = _mm_tiles(m, n, bm, bn)
    return pl.pallas_call(
        _swiglu_up_kernel,
        grid=(m // bm, n // bn),
        in_specs=[pl.BlockSpec((bm, k), lambda i, j: (i, 0)),
                  pl.BlockSpec((k, bn), lambda i, j: (0, j)),
                  pl.BlockSpec((k, bn), lambda i, j: (0, j))],
        out_specs=pl.BlockSpec((bm, bn), lambda i, j: (i, j)),
        out_shape=jax.ShapeDtypeStruct((m, n), BF16),
        compiler_params=_params("parallel", "arbitrary"),
        name="swiglu_up",
    )(x, wg, wu)


def _attn_kernel(lam_ref, q_ref, k_ref, v_ref, w_ref, o_ref, m_sc, l_sc, acc_sc, *, blk, lam_init):
    d = ATT_HEAD_DIM
    nt = blk // LANES
    qi = pl.program_id(2)
    q = q_ref[...]

    m_sc[...] = jnp.full(m_sc.shape, -jnp.inf, F32)
    l_sc[...] = jnp.zeros(l_sc.shape, F32)
    acc_sc[...] = jnp.zeros(acc_sc.shape, F32)

    def step(kb, masked):
        start = pl.multiple_of(kb * blk, blk)
        k = k_ref[pl.ds(start, blk), :]
        v = v_ref[pl.ds(start, blk), :]
        if masked:
            row = lax.broadcasted_iota(jnp.int32, (blk, blk), 0)
            col = lax.broadcasted_iota(jnp.int32, (blk, blk), 1)
            keep = col <= row
        for c in range(2):
            s = _dot_nt(q[:, c * d:(c + 1) * d], k[:, c * d:(c + 1) * d])
            if masked:
                s = jnp.where(keep, s, -jnp.inf)
            tiles = [s[:, j * LANES:(j + 1) * LANES] for j in range(nt)]
            tile_max = functools.reduce(jnp.maximum, tiles)
            m_prev = m_sc[c]
            m_new = jnp.maximum(m_prev, jnp.max(tile_max, axis=-1, keepdims=True))
            alpha = jnp.exp2(m_prev - m_new)
            ps = [jnp.exp2(tl - m_new) for tl in tiles]
            l_sc[c] = alpha * l_sc[c] + functools.reduce(jnp.add, ps)
            m_sc[c] = m_new
            pv = _dot(jnp.concatenate(ps, axis=1).astype(BF16), v)
            for j in range(pv.shape[1] // LANES):
                sl = slice(j * LANES, (j + 1) * LANES)
                acc_sc[c, :, sl] = alpha * acc_sc[c, :, sl] + pv[:, sl]

    def body(kb, carry):
        step(kb, False)
        return carry

    lax.fori_loop(0, qi, body, 0)
    step(qi, True)

    lv = lam_ref[...]
    a1 = jnp.sum(lv[0:1] * lv[1:2], axis=-1, keepdims=True)
    a2 = jnp.sum(lv[2:3] * lv[3:4], axis=-1, keepdims=True)
    lam = jnp.exp(a1) - jnp.exp(a2) + lam_init
    l1 = jnp.sum(l_sc[0], axis=-1, keepdims=True)
    l2 = jnp.sum(l_sc[1], axis=-1, keepdims=True)
    o = acc_sc[0] / l1 - lam * (acc_sc[1] / l2)
    ms = jnp.mean(o * o, axis=-1, keepdims=True)
    o_ref[...] = (o * lax.rsqrt(ms + EPS) * (w_ref[...] * (1.0 - lam_init))).astype(o_ref.dtype)


def diff_attention(qk, v, lam_vecs, subln_w, lam_init, batch, seq, blk=512):
    t, width = v.shape
    hv = 2 * ATT_HEAD_DIM
    heads = width // hv
    blk = min(blk, seq)
    nq = seq // blk
    kern = functools.partial(_attn_kernel, blk=blk, lam_init=lam_init)
    return pl.pallas_call(
        kern,
        grid=(batch, heads, nq),
        in_specs=[pl.BlockSpec((4, ATT_HEAD_DIM), lambda b, h, i: (0, 0)),
                  pl.BlockSpec((blk, hv), lambda b, h, i: (b * nq + i, h)),
                  pl.BlockSpec((seq, hv), lambda b, h, i: (b, heads + h)),
                  pl.BlockSpec((seq, hv), lambda b, h, i: (b, h)),
                  pl.BlockSpec((1, hv), lambda b, h, i: (0, 0))],
        out_specs=pl.BlockSpec((blk, hv), lambda b, h, i: (b * nq + i, h)),
        out_shape=jax.ShapeDtypeStruct((t, width), BF16),
        scratch_shapes=[pltpu.VMEM((2, blk, LANES), F32),
                        pltpu.VMEM((2, blk, LANES), F32),
                        pltpu.VMEM((2, blk, hv), F32)],
        compiler_params=_params("parallel", "parallel", "arbitrary"),
        name="diff_attention",
    )(lam_vecs, qk, qk, v, subln_w.reshape(1, hv))


def _ssd_kernel(xbc_ref, z_ref, dt_ref, cw_ref, cb_ref, dtb_ref, alog_ref, dsk_ref, nw_ref,
                o_ref, ext_sc, state_sc, *, width):
    L = SSM_CHUNK
    n = SSM_STATE
    gw = width // SSM_GROUPS
    hpg = gw // SSM_HEAD_DIM
    pad = 8
    c = pl.program_id(1)

    @pl.when(c == 0)
    def _():
        ext_sc[0:pad, :] = jnp.zeros((pad, ext_sc.shape[1]), F32)
        state_sc[...] = jnp.zeros(state_sc.shape, F32)

    ext_sc[pad:pad + L, :] = xbc_ref[...]

    def conv_silu(lo, hi):
        acc = cb_ref[:, lo:hi]
        for tap in range(CONV_K):
            acc = acc + ext_sc[pl.ds(pad - (CONV_K - 1) + tap, L), lo:hi] * cw_ref[tap:tap + 1, lo:hi]
        return _silu(acc)

    dtr = dt_ref[...] + dtb_ref[...]
    dtp = jnp.maximum(dtr, 0.0) + jnp.log1p(jnp.exp(-jnp.abs(dtr)))
    a = -jnp.exp(alog_ref[...])
    d_a = dtp * a
    row = lax.broadcasted_iota(jnp.int32, (L, L), 0)
    col = lax.broadcasted_iota(jnp.int32, (L, L), 1)
    causal = col <= row
    tril = jnp.where(causal, 1.0, 0.0).astype(BF16)
    hi_, mid_, lo_ = _split3(d_a)
    acum = _dot(tril, hi_) + _dot(tril, mid_) + _dot(tril, lo_)
    acum_t = acum.T
    dtp_t = dtp.T
    e_acum = jnp.exp(acum)
    a_last = acum[L - 1:L, :]
    to_end = jnp.exp(a_last - acum) * dtp
    e_last = jnp.exp(a_last)

    head_of_col = lax.broadcasted_iota(jnp.int32, (1, gw), 1) // SSM_HEAD_DIM

    def expand(x, g):
        out = x[:, g * hpg:g * hpg + 1]
        for hd in range(1, hpg):
            out = jnp.where(head_of_col == hd, x[:, g * hpg + hd:g * hpg + hd + 1], out)
        return jnp.broadcast_to(out, (x.shape[0], gw))

    for g in range(SSM_GROUPS):
        xg = conv_silu(g * gw, (g + 1) * gw)
        bg = conv_silu(width + g * n, width + (g + 1) * n)
        cg = conv_silu(width + SSM_GROUPS * n + g * n, width + SSM_GROUPS * n + (g + 1) * n)
        xg16 = xg.astype(BF16)
        cg16 = cg.astype(BF16)
        cb = _dot_nt(cg16, bg.astype(BF16))
        y = jnp.zeros((L, gw), F32)
        for hd in range(hpg):
            h = g * hpg + hd
            seg = acum[:, h:h + 1] - acum_t[h:h + 1, :]
            decay = jnp.exp(jnp.where(causal, seg, -jnp.inf))
            w = cb * decay * dtp_t[h:h + 1, :]
            xh = jnp.where(head_of_col == hd, xg16, jnp.zeros_like(xg16))
            y = y + _dot(w.astype(BF16), xh)
        st = state_sc[g]
        y = y + _dot(cg16, st.astype(BF16)) * expand(e_acum, g)
        xw = (xg * expand(to_end, g)).astype(BF16)
        state_sc[g] = st * expand(e_last, g) + _dot(bg.T.astype(BF16), xw)
        y = y + dsk_ref[:, g * gw:(g + 1) * gw] * xg
        gated = y * _silu(z_ref[:, g * gw:(g + 1) * gw])
        ms = jnp.mean(gated * gated, axis=-1, keepdims=True)
        o_ref[:, g * gw:(g + 1) * gw] = (gated * lax.rsqrt(ms + EPS)
                                         * nw_ref[:, g * gw:(g + 1) * gw]).astype(o_ref.dtype)

    ext_sc[0:pad, :] = xbc_ref[L - pad:L, :]


def ssd_mixer(cz, dt, conv_w, conv_b, dt_bias, a_log, d_skip, norm_w, batch, seq):
    t = cz.shape[0]
    conv_ch = conv_w.shape[1]
    width = norm_w.shape[0]
    heads = width // SSM_HEAD_DIM
    L = SSM_CHUNK
    nc = seq // L
    assert cz.shape[1] == conv_ch + width and conv_ch % width == 0
    z_blk = conv_ch // width
    pad_heads = lambda v: jnp.pad(v.reshape(1, heads), ((0, 0), (0, LANES - heads)))
    dsk = jnp.repeat(d_skip, SSM_HEAD_DIM).reshape(1, width)
    return pl.pallas_call(
        functools.partial(_ssd_kernel, width=width),
        grid=(batch, nc),
        in_specs=[pl.BlockSpec((L, conv_ch), lambda b, c: (b * nc + c, 0)),
                  pl.BlockSpec((L, width), lambda b, c: (b * nc + c, z_blk)),
                  pl.BlockSpec((L, LANES), lambda b, c: (b * nc + c, 0)),
                  pl.BlockSpec((CONV_K, conv_ch), lambda b, c: (0, 0)),
                  pl.BlockSpec((1, conv_ch), lambda b, c: (0, 0)),
                  pl.BlockSpec((1, LANES), lambda b, c: (0, 0)),
                  pl.BlockSpec((1, LANES), lambda b, c: (0, 0)),
                  pl.BlockSpec((1, width), lambda b, c: (0, 0)),
                  pl.BlockSpec((1, width), lambda b, c: (0, 0))],
        out_specs=pl.BlockSpec((L, width), lambda b, c: (b * nc + c, 0)),
        out_shape=jax.ShapeDtypeStruct((t, width), BF16),
        scratch_shapes=[pltpu.VMEM((8 + L, conv_ch), F32),
                        pltpu.VMEM((SSM_GROUPS, SSM_STATE, width // SSM_GROUPS), F32)],
        compiler_params=_params("parallel", "arbitrary"),
        name="ssd_mixer",
    )(cz, cz, dt, conv_w, conv_b.reshape(1, conv_ch), pad_heads(dt_bias), pad_heads(a_log), dsk,
      norm_w.reshape(1, width))


def _norm_router_kernel(x_ref, w_ref, rw_ref, h_ref, idx_ref, gate_ref, *, n_experts):
    x = x_ref[...]
    ms = jnp.mean(x * x, axis=-1, keepdims=True)
    hn = x * lax.rsqrt(ms + EPS) * w_ref[...]
    h_ref[...] = hn
    hh, hm, _ = _split3(hn)
    rh, rm, _ = _split3(rw_ref[...])
    logits = _dot(hh, rh) + _dot(hh, rm) + _dot(hm, rh)
    lane = lax.broadcasted_iota(jnp.int32, logits.shape, 1)
    lanef = lane.astype(F32)
    logits = jnp.where(lane < n_experts, logits, -jnp.inf)
    m1 = jnp.max(logits, axis=-1, keepdims=True)
    i1 = jnp.min(jnp.where(logits == m1, lanef, float(LANES)), axis=-1, keepdims=True)
    rest = jnp.where(lanef == i1, -jnp.inf, logits)
    m2 = jnp.max(rest, axis=-1, keepdims=True)
    i2 = jnp.min(jnp.where(rest == m2, lanef, float(LANES)), axis=-1, keepdims=True)
    e = jnp.exp(m2 - m1)
    g1 = 1.0 / (1.0 + e)
    idx_ref[...] = jnp.where(lane == 0, i1, i2).astype(jnp.int32)
    gate_ref[...] = jnp.where(lane == 0, g1, e * g1)


def norm_router(x, w, router_w, bt=256):
    t, d = x.shape
    n_experts = router_w.shape[1]
    bt = min(bt, t)
    rw = jnp.pad(router_w, ((0, 0), (0, LANES - n_experts)))
    return pl.pallas_call(
        functools.partial(_norm_router_kernel, n_experts=n_experts),
        grid=(t // bt,),
        in_specs=[pl.BlockSpec((bt, d), lambda i: (i, 0)),
                  pl.BlockSpec((1, d), lambda i: (0, 0)),
                  pl.BlockSpec((d, LANES), lambda i: (0, 0))],
        out_specs=[pl.BlockSpec((bt, d), lambda i: (i, 0)),
                   pl.BlockSpec((bt, LANES), lambda i: (i, 0)),
                   pl.BlockSpec((bt, LANES), lambda i: (i, 0))],
        out_shape=[jax.ShapeDtypeStruct((t, d), F32),
                   jax.ShapeDtypeStruct((t, LANES), jnp.int32),
                   jax.ShapeDtypeStruct((t, LANES), F32)],
        compiler_params=_params("parallel"),
        name="norm_router",
    )(x, w.reshape(1, d), rw)


def _gather_rows(idx_ref, n_rows, src_hbm, dst_ref, sem):
    def issue(r, carry):
        pltpu.make_async_copy(src_hbm.at[pl.ds(idx_ref[0, 0, r], 1), :],
                              dst_ref.at[pl.ds(r, 1), :], sem).start()
        return carry

    lax.fori_loop(0, n_rows, issue, 0)
    pltpu.make_async_copy(src_hbm.at[pl.ds(0, n_rows), :], dst_ref, sem).wait()


def _moe_up_kernel(te_ref, tv_ref, tok_ref, h_hbm, wg_ref, wu_ref, o_ref, xf_sc, xb_sc, sem):
    i = pl.program_id(0)
    j = pl.program_id(1)
    valid = tv_ref[i] != 0

    @pl.when(jnp.logical_and(valid, j == 0))
    def _():
        _gather_rows(tok_ref, xf_sc.shape[0], h_hbm, xf_sc, sem)
        xb_sc[...] = xf_sc[...].astype(BF16)

    @pl.when(valid)
    def _():
        x = xb_sc[...]
        g = _dot(x, wg_ref[...])
        u = _dot(x, wu_ref[...])
        o_ref[...] = (_silu(g) * u).astype(o_ref.dtype)

    @pl.when(jnp.logical_not(valid))
    def _():
        o_ref[...] = jnp.zeros(o_ref.shape, o_ref.dtype)


def _moe_down_kernel(te_ref, tv_ref, h_ref, wd_ref, g_ref, o_ref):
    i = pl.program_id(0)
    valid = tv_ref[i] != 0

    @pl.when(valid)
    def _():
        o_ref[...] = _dot(h_ref[...], wd_ref[...]) * g_ref[...]

    @pl.when(jnp.logical_not(valid))
    def _():
        o_ref[...] = jnp.zeros(o_ref.shape, o_ref.dtype)


def _moe_combine_kernel(pos_ref, x_ref, y_hbm, o_ref, rows_sc, sem):
    bt = x_ref.shape[0]
    _gather_rows(pos_ref, TOP_K * bt, y_hbm, rows_sc, sem)
    o_ref[...] = x_ref[...] + (rows_sc[0:bt, :] + rows_sc[bt:2 * bt, :])


def moe_ffn(xt, norm_w, router_w, w_gate, w_up, w_down, bn=512, bt=256):
    t, d = xt.shape
    n_experts, _, f = w_gate.shape
    bm = min(MOE_TILE, t)
    bt = min(bt, t)
    h, idx, gates = norm_router(xt, norm_w, router_w)

    e_flat = idx[:, :TOP_K].reshape(-1)
    g_flat = gates[:, :TOP_K].reshape(-1)
    onehot = (e_flat[:, None] == jnp.arange(n_experts, dtype=jnp.int32)[None, :]).astype(jnp.int32)
    csum = jnp.cumsum(onehot, axis=0)
    rank = jnp.sum((csum - onehot) * onehot, axis=1)
    counts = csum[-1]
    tiles_e = (counts + bm - 1) // bm
    tile_end = jnp.cumsum(tiles_e)
    row_start = (tile_end - tiles_e) * bm
    pos = row_start[e_flat] + rank
    n_tiles = (TOP_K * t) // bm + n_experts
    n_rows = n_tiles * bm
    row_token = jnp.zeros((n_rows,), jnp.int32).at[pos].set(jnp.arange(TOP_K * t, dtype=jnp.int32) // TOP_K)
    row_gate = jnp.zeros((n_rows,), F32).at[pos].set(g_flat)
    tile_ids = jnp.arange(n_tiles, dtype=jnp.int32)
    n_valid = tile_end[-1]
    tile_valid = (tile_ids < n_valid).astype(jnp.int32)
    last_valid = jnp.minimum(tile_ids, n_valid - 1)
    tile_expert = jnp.sum((last_valid[:, None] >= tile_end[None, :]).astype(jnp.int32), axis=1)
    tile_expert = jnp.minimum(tile_expert, n_experts - 1)

    nj = f // bn
    wcol = lambda i, j, te, tv, last: jnp.where(tv[i] != 0, j, last)
    hid = pl.pallas_call(
        _moe_up_kernel,
        grid_spec=pltpu.PrefetchScalarGridSpec(
            num_scalar_prefetch=2,
            grid=(n_tiles, nj),
            in_specs=[pl.BlockSpec((1, 1, bm), lambda i, j, te, tv: (i, 0, 0), memory_space=pltpu.SMEM),
                      pl.BlockSpec(memory_space=pl.ANY),
                      pl.BlockSpec((None, d, bn), lambda i, j, te, tv: (te[i], 0, wcol(i, j, te, tv, nj - 1))),
                      pl.BlockSpec((None, d, bn), lambda i, j, te, tv: (te[i], 0, wcol(i, j, te, tv, nj - 1)))],
            out_specs=pl.BlockSpec((bm, bn), lambda i, j, te, tv: (i, j)),
            scratch_shapes=[pltpu.VMEM((bm, d), F32), pltpu.VMEM((bm, d), BF16),
                            pltpu.SemaphoreType.DMA(())]),
        out_shape=jax.ShapeDtypeStruct((n_rows, f), BF16),
        compiler_params=_params("arbitrary", "arbitrary"),
        name="moe_up",
    )(tile_expert, tile_valid, row_token.reshape(n_tiles, 1, bm), h, w_gate, w_up)

    nd = d // bn
    y = pl.pallas_call(
        _moe_down_kernel,
        grid_spec=pltpu.PrefetchScalarGridSpec(
            num_scalar_prefetch=2,
            grid=(n_tiles, nd),
            in_specs=[pl.BlockSpec((bm, f), lambda i, j, te, tv: (i, 0)),
                      pl.BlockSpec((None, f, bn), lambda i, j, te, tv: (te[i], 0, wcol(i, j, te, tv, nd - 1))),
                      pl.BlockSpec((bm, 1), lambda i, j, te, tv: (i, 0))],
            out_specs=pl.BlockSpec((bm, bn), lambda i, j, te, tv: (i, j))),
        out_shape=jax.ShapeDtypeStruct((n_rows, d), F32),
        compiler_params=_params("arbitrary", "arbitrary"),
        name="moe_down",
    )(tile_expert, tile_valid, hid, w_down, row_gate.reshape(n_rows, 1))

    pos_tiles = pos.reshape(t // bt, bt, TOP_K).transpose(0, 2, 1).reshape(t // bt, 1, TOP_K * bt)
    return pl.pallas_call(
        _moe_combine_kernel,
        grid=(t // bt,),
        in_specs=[pl.BlockSpec((1, 1, TOP_K * bt), lambda i: (i, 0, 0), memory_space=pltpu.SMEM),
                  pl.BlockSpec((bt, d), lambda i: (i, 0)),
                  pl.BlockSpec(memory_space=pl.ANY)],
        out_specs=pl.BlockSpec((bt, d), lambda i: (i, 0)),
        out_shape=jax.ShapeDtypeStruct((t, d), F32),
        scratch_shapes=[pltpu.VMEM((TOP_K * bt, d), F32), pltpu.SemaphoreType.DMA(())],
        compiler_params=_params("arbitrary"),
        name="moe_combine",
    )(pos_tiles, xt, y)


def _lambda_init(layer):
    return 0.8 - 0.6 * math.exp(-0.3 * layer)


def _round_up(n, m):
    return (n + m - 1) // m * m


def kernel(x, norm1_w, w_in, w_out, q_norm_w, k_norm_w, lambda_q1, lambda_k1, lambda_q2, lambda_k2,
           subln_w, conv_w, conv_b, dt_bias, a_log, d_skip, ssm_norm_w, norm2_w, ffn_w_gate, ffn_w_up,
           ffn_w_down, router_w, moe_w_gate, moe_w_up, moe_w_down):
    b, s, d = x.shape
    t = b * s
    depth = w_in.shape[0]
    ssm_width = ssm_norm_w.shape[1]
    conv_ch = conv_w.shape[2]
    ssm_heads = dt_bias.shape[1]
    att_width = (w_in.shape[2] - ssm_width - conv_ch - ssm_heads) // 3
    n_qk_heads = att_width // ATT_HEAD_DIM
    q_scale = ATT_HEAD_DIM ** -0.5 * math.log2(math.e)

    xt = x.reshape(t, d)
    for layer in range(depth):
        wl = w_in[layer]
        z0 = 3 * att_width
        c0 = z0 + ssm_width
        w_qk = wl[:, :2 * att_width].astype(BF16)
        w_v = wl[:, 2 * att_width:z0].astype(BF16)
        w_cz = jnp.concatenate([wl[:, c0:c0 + conv_ch], wl[:, z0:c0]], axis=1).astype(BF16)
        w_dt = jnp.pad(wl[:, c0 + conv_ch:], ((0, 0), (0, LANES - ssm_heads))).astype(BF16)

        h = rmsnorm(xt, norm1_w[layer])
        gain = jnp.concatenate([jnp.tile(q_norm_w[layer] * q_scale, n_qk_heads),
                                jnp.tile(k_norm_w[layer], n_qk_heads)]).reshape(1, 2 * att_width)
        qk = matmul_headnorm(h, w_qk, gain)
        v = matmul(h, w_v, BF16, name="proj_v")
        cz = matmul(h, w_cz, F32, name="proj_ssm")
        dt = matmul(h, w_dt, F32, bn=LANES, name="proj_dt")

        lam_vecs = jnp.stack([lambda_q1[layer], lambda_k1[layer], lambda_q2[layer], lambda_k2[layer]])
        att = diff_attention(qk, v, lam_vecs, subln_w[layer], _lambda_init(layer), b, s)
        ssm = ssd_mixer(cz, dt, conv_w[layer], conv_b[layer], dt_bias[layer], a_log[layer],
                        d_skip[layer], ssm_norm_w[layer], b, s)
        mix = jnp.concatenate([att, ssm], axis=1)
        xt = matmul_residual(mix, w_out[layer].astype(BF16), xt, name="out_proj")

        i = layer // 2
        if layer % 2 == 0:
            f = ffn_w_gate.shape[2]
            fp = _round_up(f, 512)
            padc = lambda w: jnp.pad(w, ((0, 0), (0, fp - f))).astype(BF16)
            h2 = rmsnorm(xt, norm2_w[layer])
            hid = swiglu_up(h2, padc(ffn_w_gate[i]), padc(ffn_w_up[i]))
            wd = jnp.pad(ffn_w_down[i], ((0, fp - f), (0, 0))).astype(BF16)
            xt = matmul_residual(hid, wd, xt, bk=fp // 4, name="ffn_down")
        else:
            xt = moe_ffn(xt, norm2_w[layer], router_w[i], moe_w_gate[i].astype(BF16),
                         moe_w_up[i].astype(BF16), moe_w_down[i].astype(BF16))
    return xt.reshape(b, s, d)
```

```python
import functools
import math

import jax
import jax.numpy as jnp
from jax import lax
from jax.experimental import pallas as pl
from jax.experimental.pallas import tpu as pltpu

F32 = jnp.float32
BF16 = jnp.bfloat16
EPS = 1e-6
LANES = 128
VMEM_LIMIT = 56 * 1024 * 1024
VMEM_LIMIT_BIG = 60 * 1024 * 1024

ATT_HEAD_DIM = 128
SSM_HEAD_DIM = 64
SSM_GROUPS = 8
SSM_STATE = 128
CONV_K = 4
SSM_CHUNK = 128
TOP_K = 2
MOE_TILE = 512


def _params(*sem, vmem=VMEM_LIMIT):
    return pltpu.CompilerParams(dimension_semantics=sem, vmem_limit_bytes=vmem)


def _dot(a, b):
    return jnp.dot(a, b, preferred_element_type=F32)


def _dot_nt(a, b):
    return lax.dot_general(a, b, (((1,), (1,)), ((), ())), preferred_element_type=F32)


def _silu(x):
    return x / (1.0 + jnp.exp(-x))


def _split3(x):
    hi = x.astype(BF16)
    r1 = x - hi.astype(F32)
    mid = r1.astype(BF16)
    lo = (r1 - mid.astype(F32)).astype(BF16)
    return hi, mid, lo


def _rmsnorm_kernel(x_ref, w_ref, o_ref):
    x = x_ref[...]
    ms = jnp.mean(x * x, axis=-1, keepdims=True)
    o_ref[...] = (x * lax.rsqrt(ms + EPS) * w_ref[...]).astype(o_ref.dtype)


def rmsnorm(x, w, out_dtype=BF16, bt=256):
    t, d = x.shape
    bt = min(bt, t)
    return pl.pallas_call(
        _rmsnorm_kernel,
        grid=(t // bt,),
        in_specs=[pl.BlockSpec((bt, d), lambda i: (i, 0)),
                  pl.BlockSpec((1, d), lambda i: (0, 0))],
        out_specs=pl.BlockSpec((bt, d), lambda i: (i, 0)),
        out_shape=jax.ShapeDtypeStruct((t, d), out_dtype),
        compiler_params=_params("parallel"),
        name="rmsnorm",
    )(x, w.reshape(1, d))


def _cast_weight(w_ref, wb_sc):
    @pl.when(pl.program_id(1) == 0)
    def _():
        wb_sc[...] = w_ref[...].astype(BF16)


def _ws_kernel(x_ref, w_ref, o_ref, wb_sc):
    _cast_weight(w_ref, wb_sc)
    o_ref[...] = _dot(x_ref[...], wb_sc[...]).astype(o_ref.dtype)


def _ws_headnorm_kernel(x_ref, w_ref, g_ref, o_ref, wb_sc):
    _cast_weight(w_ref, wb_sc)
    acc = _dot(x_ref[...], wb_sc[...])
    for c in range(acc.shape[1] // ATT_HEAD_DIM):
        sl = slice(c * ATT_HEAD_DIM, (c + 1) * ATT_HEAD_DIM)
        a = acc[:, sl]
        ms = jnp.mean(a * a, axis=-1, keepdims=True)
        o_ref[:, sl] = (a * lax.rsqrt(ms + EPS) * g_ref[:, sl]).astype(o_ref.dtype)


def _ws_res2_kernel(x1_ref, x2_ref, w_ref, r_ref, o_ref, wb_sc):
    _cast_weight(w_ref, wb_sc)
    k1 = x1_ref.shape[1]
    o_ref[...] = (r_ref[...] + _dot(x1_ref[...], wb_sc[0:k1, :])) + _dot(x2_ref[...], wb_sc[k1:, :])


def _ws_swiglu_kernel(x_ref, wg_ref, wu_ref, o_ref, wgb_sc, wub_sc):
    _cast_weight(wg_ref, wgb_sc)
    _cast_weight(wu_ref, wub_sc)
    x = x_ref[...]
    g = _dot(x, wgb_sc[...])
    u = _dot(x, wub_sc[...])
    o_ref[...] = (_silu(g) * u).astype(o_ref.dtype)


def matmul_ws(x, w, layer, col0, ncols, out_dtype, gain=None, out_block=None, bm=1024, bn=512, name="proj"):
    m, k = x.shape
    bm = min(bm, m)
    assert m % bm == 0 and ncols % bn == 0 and col0 % bn == 0
    j0 = col0 // bn
    ob = out_block if out_block is not None else (lambda j: j)
    in_specs = [pl.BlockSpec((bm, k), lambda j, i: (i, 0)),
                pl.BlockSpec((None, k, bn), lambda j, i: (layer, 0, j0 + j))]
    args = [x, w]
    kern = _ws_kernel
    if gain is not None:
        in_specs.append(pl.BlockSpec((1, bn), lambda j, i: (0, j)))
        args.append(gain)
        kern = _ws_headnorm_kernel
    return pl.pallas_call(
        kern,
        grid=(ncols // bn, m // bm),
        in_specs=in_specs,
        out_specs=pl.BlockSpec((bm, bn), lambda j, i: (i, ob(j))),
        out_shape=jax.ShapeDtypeStruct((m, ncols), out_dtype),
        scratch_shapes=[pltpu.VMEM((k, bn), BF16)],
        compiler_params=_params("arbitrary", "arbitrary"),
        name=name,
    )(*args)


def matmul_ws_res2(x1, x2, w, layer, res, bm=1024, bn=512, name="out_proj"):
    m, k1 = x1.shape
    k2 = x2.shape[1]
    n = w.shape[2]
    bm = min(bm, m)
    assert m % bm == 0 and n % bn == 0 and w.shape[1] == k1 + k2
    return pl.pallas_call(
        _ws_res2_kernel,
        grid=(n // bn, m // bm),
        in_specs=[pl.BlockSpec((bm, k1), lambda j, i: (i, 0)),
                  pl.BlockSpec((bm, k2), lambda j, i: (i, 0)),
                  pl.BlockSpec((None, k1 + k2, bn), lambda j, i: (layer, 0, j)),
                  pl.BlockSpec((bm, bn), lambda j, i: (i, j))],
        out_specs=pl.BlockSpec((bm, bn), lambda j, i: (i, j)),
        out_shape=jax.ShapeDtypeStruct((m, n), F32),
        scratch_shapes=[pltpu.VMEM((k1 + k2, bn), BF16)],
        compiler_params=_params("arbitrary", "arbitrary"),
        name=name,
    )(x1, x2, w, res)


def swiglu_up_ws(x, wg, wu, layer, bm=1024, bn=256):
    m, k = x.shape
    n = wg.shape[2]
    bm = min(bm, m)
    assert m % bm == 0 and n % bn == 0
    return pl.pallas_call(
        _ws_swiglu_kernel,
        grid=(n // bn, m // bm),
        in_specs=[pl.BlockSpec((bm, k), lambda j, i: (i, 0)),
                  pl.BlockSpec((None, k, bn), lambda j, i: (layer, 0, j)),
                  pl.BlockSpec((None, k, bn), lambda j, i: (layer, 0, j))],
        out_specs=pl.BlockSpec((bm, bn), lambda j, i: (i, j)),
        out_shape=jax.ShapeDtypeStruct((m, n), BF16),
        scratch_shapes=[pltpu.VMEM((k, bn), BF16), pltpu.VMEM((k, bn), BF16)],
        compiler_params=_params("arbitrary", "arbitrary"),
        name="swiglu_up",
    )(x, wg, wu)


def _mm_kernel(x_ref, w_ref, o_ref):
    o_ref[...] = _dot(x_ref[...], w_ref[...]).astype(o_ref.dtype)


def _mm_res_ktiled_kernel(x_ref, w_ref, r_ref, o_ref):
    k = pl.program_id(2)

    @pl.when(k == 0)
    def _():
        o_ref[...] = r_ref[...] + _dot(x_ref[...], w_ref[...])

    @pl.when(k != 0)
    def _():
        o_ref[...] += _dot(x_ref[...], w_ref[...])


def matmul(x, w, out_dtype, bm=1024, bn=512, name="matmul"):
    m, k = x.shape
    n = w.shape[1]
    bm, bn = min(bm, m), min(bn, n)
    assert m % bm == 0 and n % bn == 0
    return pl.pallas_call(
        _mm_kernel,
        grid=(m // bm, n // bn),
        in_specs=[pl.BlockSpec((bm, k), lambda i, j: (i, 0)),
                  pl.BlockSpec((k, bn), lambda i, j: (0, j))],
        out_specs=pl.BlockSpec((bm, bn), lambda i, j: (i, j)),
        out_shape=jax.ShapeDtypeStruct((m, n), out_dtype),
        compiler_params=_params("parallel", "arbitrary"),
        name=name,
    )(x, w)


def matmul_residual_ktiled(x, w, res, bk, bm=1024, bn=512, name="matmul_residual"):
    m, k = x.shape
    n = w.shape[1]
    bm, bn = min(bm, m), min(bn, n)
    assert m % bm == 0 and n % bn == 0 and k % bk == 0
    return pl.pallas_call(
        _mm_res_ktiled_kernel,
        grid=(m // bm, n // bn, k // bk),
        in_specs=[pl.BlockSpec((bm, bk), lambda i, j, kk: (i, kk)),
                  pl.BlockSpec((bk, bn), lambda i, j, kk: (kk, j)),
                  pl.BlockSpec((bm, bn), lambda i, j, kk: (i, j))],
        out_specs=pl.BlockSpec((bm, bn), lambda i, j, kk: (i, j)),
        out_shape=jax.ShapeDtypeStruct((m, n), F32),
        compiler_params=_params("parallel", "arbitrary", "arbitrary"),
        name=name,
    )(x, w, res)


def _attn_kernel(lam_ref, q_ref, k_ref, v_ref, w_ref, o_ref, m_sc, l_sc, acc_sc, *, blk, nsub, lam_init):
    d = ATT_HEAD_DIM
    nt = blk // LANES
    qi = pl.program_id(2)

    m_sc[...] = jnp.full(m_sc.shape, -jnp.inf, F32)
    l_sc[...] = jnp.zeros(l_sc.shape, F32)
    acc_sc[...] = jnp.zeros(acc_sc.shape, F32)

    row = lax.broadcasted_iota(jnp.int32, (blk, blk), 0)
    col = lax.broadcasted_iota(jnp.int32, (blk, blk), 1)
    keep = col <= row

    def chain(sub, c, k, v, masked):
        ch = sub * 2 + c
        q = q_ref[sub * blk:(sub + 1) * blk, c * d:(c + 1) * d]
        s = _dot_nt(q, k[:, c * d:(c + 1) * d])
        if masked:
            s = jnp.where(keep, s, -jnp.inf)
        tiles = [s[:, j * LANES:(j + 1) * LANES] for j in range(nt)]
        tile_max = functools.reduce(jnp.maximum, tiles)
        m_prev = m_sc[ch]
        m_new = jnp.maximum(m_prev, jnp.max(tile_max, axis=-1, keepdims=True))
        alpha = jnp.exp2(m_prev - m_new)
        ps = [jnp.exp2(tl - m_new) for tl in tiles]
        l_sc[ch] = alpha * l_sc[ch] + functools.reduce(jnp.add, ps)
        m_sc[ch] = m_new
        pv = _dot(jnp.concatenate(ps, axis=1).astype(BF16), v)
        for j in range(pv.shape[1] // LANES):
            sl = slice(j * LANES, (j + 1) * LANES)
            acc_sc[ch, :, sl] = alpha * acc_sc[ch, :, sl] + pv[:, sl]

    def step(kb, first_sub, diag_sub):
        start = pl.multiple_of(kb * blk, blk)
        k = k_ref[pl.ds(start, blk), :]
        v = v_ref[pl.ds(start, blk), :]
        for sub in range(first_sub, nsub):
            for c in range(2):
                chain(sub, c, k, v, sub == diag_sub)

    def body(kb, carry):
        step(kb, 0, None)
        return carry

    lax.fori_loop(0, qi * nsub, body, 0)
    for sub in range(nsub):
        step(qi * nsub + sub, sub, sub)

    lv = lam_ref[...]
    a1 = jnp.sum(lv[0:1] * lv[1:2], axis=-1, keepdims=True)
    a2 = jnp.sum(lv[2:3] * lv[3:4], axis=-1, keepdims=True)
    lam = jnp.exp(a1) - jnp.exp(a2) + lam_init
    gain = w_ref[...] * (1.0 - lam_init)
    for sub in range(nsub):
        l1 = jnp.sum(l_sc[2 * sub], axis=-1, keepdims=True)
        l2 = jnp.sum(l_sc[2 * sub + 1], axis=-1, keepdims=True)
        o = acc_sc[2 * sub] / l1 - lam * (acc_sc[2 * sub + 1] / l2)
        ms = jnp.mean(o * o, axis=-1, keepdims=True)
        o_ref[sub * blk:(sub + 1) * blk, :] = (o * lax.rsqrt(ms + EPS) * gain).astype(o_ref.dtype)


def diff_attention(qk, v, lam_vecs, subln_w, lam_init, batch, seq, blk=512, nsub=4):
    t, width = v.shape
    hv = 2 * ATT_HEAD_DIM
    heads = width // hv
    blk = min(blk, seq)
    nsub = min(nsub, seq // blk)
    bq = blk * nsub
    nq = seq // bq
    kern = functools.partial(_attn_kernel, blk=blk, nsub=nsub, lam_init=lam_init)
    return pl.pallas_call(
        kern,
        grid=(batch, heads, nq),
        in_specs=[pl.BlockSpec((4, ATT_HEAD_DIM), lambda b, h, i: (0, 0)),
                  pl.BlockSpec((bq, hv), lambda b, h, i: (b * nq + i, h)),
                  pl.BlockSpec((seq, hv), lambda b, h, i: (b, heads + h)),
                  pl.BlockSpec((seq, hv), lambda b, h, i: (b, h)),
                  pl.BlockSpec((1, hv), lambda b, h, i: (0, 0))],
        out_specs=pl.BlockSpec((bq, hv), lambda b, h, i: (b * nq + i, h)),
        out_shape=jax.ShapeDtypeStruct((t, width), BF16),
        scratch_shapes=[pltpu.VMEM((2 * nsub, blk, LANES), F32),
                        pltpu.VMEM((2 * nsub, blk, LANES), F32),
                        pltpu.VMEM((2 * nsub, blk, hv), F32)],
        compiler_params=_params("parallel", "parallel", "arbitrary"),
        name="diff_attention",
    )(lam_vecs, qk, qk, v, subln_w.reshape(1, hv))


def _ssd_kernel(xbc_ref, z_ref, dt_ref, cw_ref, cb_ref, dtb_ref, alog_ref, dsk_ref, nw_ref,
                o_ref, ext_sc, state_sc, *, width):
    L = SSM_CHUNK
    n = SSM_STATE
    gw = width // SSM_GROUPS
    hpg = gw // SSM_HEAD_DIM
    pad = 8
    c = pl.program_id(1)

    @pl.when(c == 0)
    def _():
        ext_sc[0:pad, :] = jnp.zeros((pad, ext_sc.shape[1]), F32)
        state_sc[...] = jnp.zeros(state_sc.shape, F32)

    ext_sc[pad:pad + L, :] = xbc_ref[...]

    def conv_silu(lo, hi):
        acc = cb_ref[:, lo:hi]
        for tap in range(CONV_K):
            acc = acc + ext_sc[pl.ds(pad - (CONV_K - 1) + tap, L), lo:hi] * cw_ref[tap:tap + 1, lo:hi]
        return _silu(acc)

    dtr = dt_ref[...] + dtb_ref[...]
    dtp = jnp.maximum(dtr, 0.0) + jnp.log1p(jnp.exp(-jnp.abs(dtr)))
    a = -jnp.exp(alog_ref[...])
    d_a = dtp * a
    row = lax.broadcasted_iota(jnp.int32, (L, L), 0)
    col = lax.broadcasted_iota(jnp.int32, (L, L), 1)
    causal = col <= row
    tril = jnp.where(causal, 1.0, 0.0).astype(BF16)
    hi_, mid_, lo_ = _split3(d_a)
    acum = _dot(tril, hi_) + _dot(tril, mid_) + _dot(tril, lo_)
    acum_t = acum.T
    dtp_t = dtp.T
    e_acum = jnp.exp(acum)
    a_last = acum[L - 1:L, :]
    to_end = jnp.exp(a_last - acum) * dtp
    e_last = jnp.exp(a_last)

    head_of_col = lax.broadcasted_iota(jnp.int32, (1, gw), 1) // SSM_HEAD_DIM

    def expand(x, g):
        out = x[:, g * hpg:g * hpg + 1]
        for hd in range(1, hpg):
            out = jnp.where(head_of_col == hd, x[:, g * hpg + hd:g * hpg + hd + 1], out)
        return jnp.broadcast_to(out, (x.shape[0], gw))

    for g in range(SSM_GROUPS):
        xg = conv_silu(g * gw, (g + 1) * gw)
        bg = conv_silu(width + g * n, width + (g + 1) * n)
        cg = conv_silu(width + SSM_GROUPS * n + g * n, width + SSM_GROUPS * n + (g + 1) * n)
        xg16 = xg.astype(BF16)
        cg16 = cg.astype(BF16)
        cb = _dot_nt(cg16, bg.astype(BF16))
        y = jnp.zeros((L, gw), F32)
        for hd in range(hpg):
            h = g * hpg + hd
            seg = acum[:, h:h + 1] - acum_t[h:h + 1, :]
            decay = jnp.exp(jnp.where(causal, seg, -jnp.inf))
            w = cb * decay * dtp_t[h:h + 1, :]
            xh = jnp.where(head_of_col == hd, xg16, jnp.zeros_like(xg16))
            y = y + _dot(w.astype(BF16), xh)
        st = state_sc[g]
        y = y + _dot(cg16, st.astype(BF16)) * expand(e_acum, g)
        xw = (xg * expand(to_end, g)).astype(BF16)
        state_sc[g] = st * expand(e_last, g) + _dot(bg.T.astype(BF16), xw)
        y = y + dsk_ref[:, g * gw:(g + 1) * gw] * xg
        gated = y * _silu(z_ref[:, g * gw:(g + 1) * gw])
        ms = jnp.mean(gated * gated, axis=-1, keepdims=True)
        o_ref[:, g * gw:(g + 1) * gw] = (gated * lax.rsqrt(ms + EPS)
                                         * nw_ref[:, g * gw:(g + 1) * gw]).astype(o_ref.dtype)

    ext_sc[0:pad, :] = xbc_ref[L - pad:L, :]


def ssd_mixer(cz, dt, conv_w, conv_b, dt_bias, a_log, d_skip, norm_w, batch, seq):
    t = cz.shape[0]
    conv_ch = conv_w.shape[1]
    width = norm_w.shape[0]
    heads = width // SSM_HEAD_DIM
    L = SSM_CHUNK
    nc = seq // L
    assert cz.shape[1] == conv_ch + width and conv_ch % width == 0
    z_blk = conv_ch // width
    pad_heads = lambda v: jnp.pad(v.reshape(1, heads), ((0, 0), (0, LANES - heads)))
    dsk = jnp.repeat(d_skip, SSM_HEAD_DIM).reshape(1, width)
    return pl.pallas_call(
        functools.partial(_ssd_kernel, width=width),
        grid=(batch, nc),
        in_specs=[pl.BlockSpec((L, conv_ch), lambda b, c: (b * nc + c, 0)),
                  pl.BlockSpec((L, width), lambda b, c: (b * nc + c, z_blk)),
                  pl.BlockSpec((L, LANES), lambda b, c: (b * nc + c, 0)),
                  pl.BlockSpec((CONV_K, conv_ch), lambda b, c: (0, 0)),
                  pl.BlockSpec((1, conv_ch), lambda b, c: (0, 0)),
                  pl.BlockSpec((1, LANES), lambda b, c: (0, 0)),
                  pl.BlockSpec((1, LANES), lambda b, c: (0, 0)),
                  pl.BlockSpec((1, width), lambda b, c: (0, 0)),
                  pl.BlockSpec((1, width), lambda b, c: (0, 0))],
        out_specs=pl.BlockSpec((L, width), lambda b, c: (b * nc + c, 0)),
        out_shape=jax.ShapeDtypeStruct((t, width), BF16),
        scratch_shapes=[pltpu.VMEM((8 + L, conv_ch), F32),
                        pltpu.VMEM((SSM_GROUPS, SSM_STATE, width // SSM_GROUPS), F32)],
        compiler_params=_params("parallel", "arbitrary"),
        name="ssd_mixer",
    )(cz, cz, dt, conv_w, conv_b.reshape(1, conv_ch), pad_heads(dt_bias), pad_heads(a_log), dsk,
      norm_w.reshape(1, width))


def _norm_router_kernel(x_ref, w_ref, rw_ref, h_ref, idx_ref, gate_ref, *, n_experts):
    x = x_ref[...]
    ms = jnp.mean(x * x, axis=-1, keepdims=True)
    hn = x * lax.rsqrt(ms + EPS) * w_ref[...]
    h_ref[...] = hn
    hh, hm, _ = _split3(hn)
    rh, rm, _ = _split3(rw_ref[...])
    logits = _dot(hh, rh) + _dot(hh, rm) + _dot(hm, rh)
    lane = lax.broadcasted_iota(jnp.int32, logits.shape, 1)
    lanef = lane.astype(F32)
    logits = jnp.where(lane < n_experts, logits, -jnp.inf)
    m1 = jnp.max(logits, axis=-1, keepdims=True)
    i1 = jnp.min(jnp.where(logits == m1, lanef, float(LANES)), axis=-1, keepdims=True)
    rest = jnp.where(lanef == i1, -jnp.inf, logits)
    m2 = jnp.max(rest, axis=-1, keepdims=True)
    i2 = jnp.min(jnp.where(rest == m2, lanef, float(LANES)), axis=-1, keepdims=True)
    e = jnp.exp(m2 - m1)
    g1 = 1.0 / (1.0 + e)
    idx_ref[...] = jnp.where(lane == 0, i1, i2).astype(jnp.int32)
    gate_ref[...] = jnp.where(lane == 0, g1, e * g1)


def norm_router(x, w, router_w, bt=256):
    t, d = x.shape
    n_experts = router_w.shape[1]
    bt = min(bt, t)
    rw = jnp.pad(router_w, ((0, 0), (0, LANES - n_experts)))
    return pl.pallas_call(
        functools.partial(_norm_router_kernel, n_experts=n_experts),
        grid=(t // bt,),
        in_specs=[pl.BlockSpec((bt, d), lambda i: (i, 0)),
                  pl.BlockSpec((1, d), lambda i: (0, 0)),
                  pl.BlockSpec((d, LANES), lambda i: (0, 0))],
        out_specs=[pl.BlockSpec((bt, d), lambda i: (i, 0)),
                   pl.BlockSpec((bt, LANES), lambda i: (i, 0)),
                   pl.BlockSpec((bt, LANES), lambda i: (i, 0))],
        out_shape=[jax.ShapeDtypeStruct((t, d), F32),
                   jax.ShapeDtypeStruct((t, LANES), jnp.int32),
                   jax.ShapeDtypeStruct((t, LANES), F32)],
        compiler_params=_params("parallel"),
        name="norm_router",
    )(x, w.reshape(1, d), rw)


def _issue_row_gather(idx_ref, n_rows, src_hbm, dst_ref, sem):
    def issue(r, carry):
        pltpu.make_async_copy(src_hbm.at[pl.ds(idx_ref[0, 0, r], 1), :],
                              dst_ref.at[pl.ds(r, 1), :], sem).start()
        return carry

    lax.fori_loop(0, n_rows, issue, 0)


def _wait_row_gather(n_rows, src_hbm, dst_ref, sem):
    pltpu.make_async_copy(src_hbm.at[pl.ds(0, n_rows), :], dst_ref, sem).wait()


def _moe_gather_kernel(tv_ref, tok_ref, tok_next_ref, h_hbm, o_ref, rows_sc, sems):
    i = pl.program_id(0)
    n = pl.num_programs(0)
    bm = o_ref.shape[0]
    slot = lax.rem(i, 2)

    @pl.when(jnp.logical_and(i == 0, tv_ref[0] != 0))
    def _():
        _issue_row_gather(tok_ref, bm, h_hbm, rows_sc.at[0], sems.at[0])

    nxt = jnp.minimum(i + 1, n - 1)

    @pl.when(jnp.logical_and(i + 1 < n, tv_ref[nxt] != 0))
    def _():
        _issue_row_gather(tok_next_ref, bm, h_hbm, rows_sc.at[1 - slot], sems.at[1 - slot])

    @pl.when(tv_ref[i] != 0)
    def _():
        _wait_row_gather(bm, h_hbm, rows_sc.at[slot], sems.at[slot])
        o_ref[...] = rows_sc[slot].astype(o_ref.dtype)

    @pl.when(tv_ref[i] == 0)
    def _():
        o_ref[...] = jnp.zeros(o_ref.shape, o_ref.dtype)


def _expert_changed(te_ref, i):
    return jnp.logical_or(i == 0, te_ref[i] != te_ref[jnp.maximum(i - 1, 0)])


def _moe_up_kernel(te_ref, tv_ref, x_ref, wg_ref, wu_ref, o_ref, wgb_sc, wub_sc):
    i = pl.program_id(1)
    valid = tv_ref[i] != 0

    @pl.when(_expert_changed(te_ref, i))
    def _():
        wgb_sc[...] = wg_ref[...].astype(BF16)
        wub_sc[...] = wu_ref[...].astype(BF16)

    @pl.when(valid)
    def _():
        x = x_ref[...]
        g = _dot(x, wgb_sc[...])
        u = _dot(x, wub_sc[...])
        o_ref[...] = (_silu(g) * u).astype(o_ref.dtype)

    @pl.when(jnp.logical_not(valid))
    def _():
        o_ref[...] = jnp.zeros(o_ref.shape, o_ref.dtype)


def _moe_down_kernel(te_ref, tv_ref, h_ref, wd_ref, o_ref, wdb_sc):
    i = pl.program_id(1)
    valid = tv_ref[i] != 0

    @pl.when(_expert_changed(te_ref, i))
    def _():
        wdb_sc[...] = wd_ref[...].astype(BF16)

    @pl.when(valid)
    def _():
        o_ref[...] = _dot(h_ref[...], wdb_sc[...])

    @pl.when(jnp.logical_not(valid))
    def _():
        o_ref[...] = jnp.zeros(o_ref.shape, o_ref.dtype)


def _moe_combine_kernel(pos_ref, x_ref, g_ref, y_hbm, o_ref, rows_sc, sem):
    bt = x_ref.shape[0]
    _issue_row_gather(pos_ref, TOP_K * bt, y_hbm, rows_sc, sem)
    _wait_row_gather(TOP_K * bt, y_hbm, rows_sc, sem)
    g = g_ref[...]
    o_ref[...] = x_ref[...] + (g[:, 0:1] * rows_sc[0:bt, :] + g[:, 1:2] * rows_sc[bt:2 * bt, :])


def moe_ffn(xt, norm_w, router_w, w_gate, w_up, w_down, mi, bn=512, bt=256):
    t, d = xt.shape
    _, n_experts, _, f = w_gate.shape
    bm = min(MOE_TILE, t)
    bt = min(bt, t)
    h, idx, gates = norm_router(xt, norm_w, router_w)

    e_flat = idx[:, :TOP_K].reshape(-1)
    onehot = (e_flat[:, None] == jnp.arange(n_experts, dtype=jnp.int32)[None, :]).astype(jnp.int32)
    csum = jnp.cumsum(onehot, axis=0)
    rank = jnp.sum((csum - onehot) * onehot, axis=1)
    counts = csum[-1]
    tiles_e = (counts + bm - 1) // bm
    tile_end = jnp.cumsum(tiles_e)
    row_start = (tile_end - tiles_e) * bm
    pos = row_start[e_flat] + rank
    n_tiles = (TOP_K * t) // bm + n_experts
    n_rows = n_tiles * bm
    row_token = jnp.zeros((n_rows,), jnp.int32).at[pos].set(jnp.arange(TOP_K * t, dtype=jnp.int32) // TOP_K)
    tile_ids = jnp.arange(n_tiles, dtype=jnp.int32)
    n_valid = tile_end[-1]
    tile_valid = (tile_ids < n_valid).astype(jnp.int32)
    last_valid = jnp.minimum(tile_ids, n_valid - 1)
    tile_expert = jnp.sum((last_valid[:, None] >= tile_end[None, :]).astype(jnp.int32), axis=1)
    tile_expert = jnp.minimum(tile_expert, n_experts - 1)

    tok_tiles = row_token.reshape(n_tiles, 1, bm)
    xs = pl.pallas_call(
        _moe_gather_kernel,
        grid_spec=pltpu.PrefetchScalarGridSpec(
            num_scalar_prefetch=1,
            grid=(n_tiles,),
            in_specs=[pl.BlockSpec((1, 1, bm), lambda i, tv: (i, 0, 0), memory_space=pltpu.SMEM),
                      pl.BlockSpec((1, 1, bm), lambda i, tv: (jnp.minimum(i + 1, n_tiles - 1), 0, 0),
                                   memory_space=pltpu.SMEM),
                      pl.BlockSpec(memory_space=pl.ANY)],
            out_specs=pl.BlockSpec((bm, d), lambda i, tv: (i, 0)),
            scratch_shapes=[pltpu.VMEM((2, bm, d), F32), pltpu.SemaphoreType.DMA((2,))]),
        out_shape=jax.ShapeDtypeStruct((n_rows, d), BF16),
        compiler_params=_params("arbitrary"),
        name="moe_gather",
    )(tile_valid, tok_tiles, tok_tiles, h)

    hid = pl.pallas_call(
        _moe_up_kernel,
        grid_spec=pltpu.PrefetchScalarGridSpec(
            num_scalar_prefetch=2,
            grid=(f // bn, n_tiles),
            in_specs=[pl.BlockSpec((bm, d), lambda j, i, te, tv: (i, 0)),
                      pl.BlockSpec((None, None, d, bn), lambda j, i, te, tv: (mi, te[i], 0, j)),
                      pl.BlockSpec((None, None, d, bn), lambda j, i, te, tv: (mi, te[i], 0, j))],
            out_specs=pl.BlockSpec((bm, bn), lambda j, i, te, tv: (i, j)),
            scratch_shapes=[pltpu.VMEM((d, bn), BF16), pltpu.VMEM((d, bn), BF16)]),
        out_shape=jax.ShapeDtypeStruct((n_rows, f), BF16),
        compiler_params=_params("arbitrary", "arbitrary", vmem=VMEM_LIMIT_BIG),
        name="moe_up",
    )(tile_expert, tile_valid, xs, w_gate, w_up)

    y = pl.pallas_call(
        _moe_down_kernel,
        grid_spec=pltpu.PrefetchScalarGridSpec(
            num_scalar_prefetch=2,
            grid=(d // bn, n_tiles),
            in_specs=[pl.BlockSpec((bm, f), lambda j, i, te, tv: (i, 0)),
                      pl.BlockSpec((None, None, f, bn), lambda j, i, te, tv: (mi, te[i], 0, j))],
            out_specs=pl.BlockSpec((bm, bn), lambda j, i, te, tv: (i, j)),
            scratch_shapes=[pltpu.VMEM((f, bn), BF16)]),
        out_shape=jax.ShapeDtypeStruct((n_rows, d), F32),
        compiler_params=_params("arbitrary", "arbitrary"),
        name="moe_down",
    )(tile_expert, tile_valid, hid, w_down)

    pos_tiles = pos.reshape(t // bt, bt, TOP_K).transpose(0, 2, 1).reshape(t // bt, 1, TOP_K * bt)
    return pl.pallas_call(
        _moe_combine_kernel,
        grid=(t // bt,),
        in_specs=[pl.BlockSpec((1, 1, TOP_K * bt), lambda i: (i, 0, 0), memory_space=pltpu.SMEM),
                  pl.BlockSpec((bt, d), lambda i: (i, 0)),
                  pl.BlockSpec((bt, LANES), lambda i: (i, 0)),
                  pl.BlockSpec(memory_space=pl.ANY)],
        out_specs=pl.BlockSpec((bt, d), lambda i: (i, 0)),
        out_shape=jax.ShapeDtypeStruct((t, d), F32),
        scratch_shapes=[pltpu.VMEM((TOP_K * bt, d), F32), pltpu.SemaphoreType.DMA(())],
        compiler_params=_params("arbitrary"),
        name="moe_combine",
    )(pos_tiles, xt, gates, y)


def _lambda_init(layer):
    return 0.8 - 0.6 * math.exp(-0.3 * layer)


def kernel(x, norm1_w, w_in, w_out, q_norm_w, k_norm_w, lambda_q1, lambda_k1, lambda_q2, lambda_k2,
           subln_w, conv_w, conv_b, dt_bias, a_log, d_skip, ssm_norm_w, norm2_w, ffn_w_gate, ffn_w_up,
           ffn_w_down, router_w, moe_w_gate, moe_w_up, moe_w_down):
    b, s, d = x.shape
    t = b * s
    depth = w_in.shape[0]
    ssm_width = ssm_norm_w.shape[1]
    conv_ch = conv_w.shape[2]
    ssm_heads = dt_bias.shape[1]
    att_width = (w_in.shape[2] - ssm_width - conv_ch - ssm_heads) // 3
    n_qk_heads = att_width // ATT_HEAD_DIM
    q_scale = ATT_HEAD_DIM ** -0.5 * math.log2(math.e)
    bn = 512
    z0 = 3 * att_width
    c0 = z0 + ssm_width
    nz, nx = ssm_width // bn, conv_ch // bn

    xt = x.reshape(t, d)
    for layer in range(depth):
        w_dt = jnp.pad(w_in[layer, :, c0 + conv_ch:], ((0, 0), (0, LANES - ssm_heads))).astype(BF16)

        h = rmsnorm(xt, norm1_w[layer])
        gain = jnp.concatenate([jnp.tile(q_norm_w[layer] * q_scale, n_qk_heads),
                                jnp.tile(k_norm_w[layer], n_qk_heads)]).reshape(1, 2 * att_width)
        qk = matmul_ws(h, w_in, layer, 0, 2 * att_width, BF16, gain=gain, bn=bn, name="proj_qk")
        v = matmul_ws(h, w_in, layer, 2 * att_width, att_width, BF16, bn=bn, name="proj_v")
        cz = matmul_ws(h, w_in, layer, z0, ssm_width + conv_ch, F32, bn=bn, name="proj_ssm",
                       out_block=lambda j: jnp.where(j < nz, j + nx, j - nz))
        dt = matmul(h, w_dt, F32, bn=LANES, name="proj_dt")

        lam_vecs = jnp.stack([lambda_q1[layer], lambda_k1[layer], lambda_q2[layer], lambda_k2[layer]])
        att = diff_attention(qk, v, lam_vecs, subln_w[layer], _lambda_init(layer), b, s)
        ssm = ssd_mixer(cz, dt, conv_w[layer], conv_b[layer], dt_bias[layer], a_log[layer],
                        d_skip[layer], ssm_norm_w[layer], b, s)
        xt = matmul_ws_res2(att, ssm, w_out, layer, xt, bn=bn)

        i = layer // 2
        if layer % 2 == 0:
            f = ffn_w_gate.shape[2]
            h2 = rmsnorm(xt, norm2_w[layer])
            hid = swiglu_up_ws(h2, ffn_w_gate, ffn_w_up, i)
            xt = matmul_residual_ktiled(hid, ffn_w_down[i].astype(BF16), xt, bk=f // 2, name="ffn_down")
        else:
            xt = moe_ffn(xt, norm2_w[layer], router_w[i], moe_w_gate, moe_w_up, moe_w_down, i)
    return xt.reshape(b, s, d)
```

```python
import functools
import math

import jax
import jax.numpy as jnp
from jax import lax
from jax.experimental import pallas as pl
from jax.experimental.pallas import tpu as pltpu

F32 = jnp.float32
BF16 = jnp.bfloat16
EPS = 1e-6
LANES = 128
VMEM_LIMIT = 56 * 1024 * 1024
VMEM_LIMIT_BIG = 60 * 1024 * 1024

ATT_HEAD_DIM = 128
SSM_HEAD_DIM = 64
SSM_GROUPS = 8
SSM_STATE = 128
CONV_K = 4
SSM_CHUNK = 128
TOP_K = 2
MOE_TILE = 512


def _params(*sem, vmem=VMEM_LIMIT):
    return pltpu.CompilerParams(dimension_semantics=sem, vmem_limit_bytes=vmem)


def _dot(a, b):
    return jnp.dot(a, b, preferred_element_type=F32)


def _dot_nt(a, b):
    return lax.dot_general(a, b, (((1,), (1,)), ((), ())), preferred_element_type=F32)


def _silu(x):
    return x / (1.0 + jnp.exp(-x))


def _split3(x):
    hi = x.astype(BF16)
    r1 = x - hi.astype(F32)
    mid = r1.astype(BF16)
    lo = (r1 - mid.astype(F32)).astype(BF16)
    return hi, mid, lo


def _rmsnorm_kernel(x_ref, w_ref, o_ref):
    x = x_ref[...]
    ms = jnp.mean(x * x, axis=-1, keepdims=True)
    o_ref[...] = (x * lax.rsqrt(ms + EPS) * w_ref[...]).astype(o_ref.dtype)


def rmsnorm(x, w, out_dtype=BF16, bt=256):
    t, d = x.shape
    bt = min(bt, t)
    return pl.pallas_call(
        _rmsnorm_kernel,
        grid=(t // bt,),
        in_specs=[pl.BlockSpec((bt, d), lambda i: (i, 0)),
                  pl.BlockSpec((1, d), lambda i: (0, 0))],
        out_specs=pl.BlockSpec((bt, d), lambda i: (i, 0)),
        out_shape=jax.ShapeDtypeStruct((t, d), out_dtype),
        compiler_params=_params("parallel"),
        name="rmsnorm",
    )(x, w.reshape(1, d))


ONE_BUFFER = pl.Buffered(1)


def _cast_weight(w_ref, wb_sc):
    @pl.when(pl.program_id(1) == 0)
    def _():
        wb_sc[...] = w_ref[...].astype(BF16)


def _ws_kernel(x_ref, w_ref, o_ref, wb_sc, *, dot):
    _cast_weight(w_ref, wb_sc)
    o_ref[...] = dot(x_ref[...], wb_sc[...]).astype(o_ref.dtype)


def _ws_headnorm_kernel(x_ref, w_ref, g_ref, o_ref, wb_sc, *, dot):
    _cast_weight(w_ref, wb_sc)
    acc = dot(x_ref[...], wb_sc[...])
    for c in range(acc.shape[1] // ATT_HEAD_DIM):
        sl = slice(c * ATT_HEAD_DIM, (c + 1) * ATT_HEAD_DIM)
        a = acc[:, sl]
        ms = jnp.mean(a * a, axis=-1, keepdims=True)
        o_ref[:, sl] = (a * lax.rsqrt(ms + EPS) * g_ref[:, sl]).astype(o_ref.dtype)


def _ws_res2_kernel(x1_ref, x2_ref, w_ref, r_ref, o_ref, wb_sc):
    _cast_weight(w_ref, wb_sc)
    k1 = x1_ref.shape[1]
    o_ref[...] = (r_ref[...] + _dot(x1_ref[...], wb_sc[0:k1, :])) + _dot(x2_ref[...], wb_sc[k1:, :])


def _ws_swiglu_kernel(x_ref, wg_ref, wu_ref, o_ref, wgb_sc, wub_sc):
    _cast_weight(wg_ref, wgb_sc)
    _cast_weight(wu_ref, wub_sc)
    x = x_ref[...]
    g = _dot(x, wgb_sc[...])
    u = _dot(x, wub_sc[...])
    o_ref[...] = (_silu(g) * u).astype(o_ref.dtype)


def matmul_ws(x, wt, layer, col0, ncols, out_dtype, gain=None, out_block=None, bm=2048, bn=512, name="proj"):
    m, k = x.shape
    bm = min(bm, m)
    assert m % bm == 0 and ncols % bn == 0 and col0 % bn == 0
    j0 = col0 // bn
    ob = out_block if out_block is not None else (lambda j: j)
    in_specs = [pl.BlockSpec((bm, k), lambda j, i: (i, 0)),
                pl.BlockSpec((None, bn, k), lambda j, i: (layer, j0 + j, 0), pipeline_mode=ONE_BUFFER)]
    args = [x, wt]
    kern = _ws_kernel
    if gain is not None:
        in_specs.append(pl.BlockSpec((1, bn), lambda j, i: (0, j)))
        args.append(gain)
        kern = _ws_headnorm_kernel
    return pl.pallas_call(
        functools.partial(kern, dot=_dot_nt),
        grid=(ncols // bn, m // bm),
        in_specs=in_specs,
        out_specs=pl.BlockSpec((bm, bn), lambda j, i: (i, ob(j))),
        out_shape=jax.ShapeDtypeStruct((m, ncols), out_dtype),
        scratch_shapes=[pltpu.VMEM((bn, k), BF16)],
        compiler_params=_params("arbitrary", "arbitrary", vmem=VMEM_LIMIT_BIG),
        name=name,
    )(*args)


def matmul_ws_res2(x1, x2, w, layer, res, bm=1024, bn=512, name="out_proj"):
    m, k1 = x1.shape
    k2 = x2.shape[1]
    n = w.shape[2]
    bm = min(bm, m)
    assert m % bm == 0 and n % bn == 0 and w.shape[1] == k1 + k2
    return pl.pallas_call(
        _ws_res2_kernel,
        grid=(n // bn, m // bm),
        in_specs=[pl.BlockSpec((bm, k1), lambda j, i: (i, 0)),
                  pl.BlockSpec((bm, k2), lambda j, i: (i, 0)),
                  pl.BlockSpec((None, k1 + k2, bn), lambda j, i: (layer, 0, j), pipeline_mode=ONE_BUFFER),
                  pl.BlockSpec((bm, bn), lambda j, i: (i, j))],
        out_specs=pl.BlockSpec((bm, bn), lambda j, i: (i, j)),
        out_shape=jax.ShapeDtypeStruct((m, n), F32),
        scratch_shapes=[pltpu.VMEM((k1 + k2, bn), BF16)],
        compiler_params=_params("arbitrary", "arbitrary"),
        name=name,
    )(x1, x2, w, res)


def swiglu_up_ws(x, wg, wu, layer, bm=2048, bn=256):
    m, k = x.shape
    n = wg.shape[2]
    bm = min(bm, m)
    assert m % bm == 0 and n % bn == 0
    return pl.pallas_call(
        _ws_swiglu_kernel,
        grid=(n // bn, m // bm),
        in_specs=[pl.BlockSpec((bm, k), lambda j, i: (i, 0)),
                  pl.BlockSpec((None, k, bn), lambda j, i: (layer, 0, j), pipeline_mode=ONE_BUFFER),
                  pl.BlockSpec((None, k, bn), lambda j, i: (layer, 0, j), pipeline_mode=ONE_BUFFER)],
        out_specs=pl.BlockSpec((bm, bn), lambda j, i: (i, j)),
        out_shape=jax.ShapeDtypeStruct((m, n), BF16),
        scratch_shapes=[pltpu.VMEM((k, bn), BF16), pltpu.VMEM((k, bn), BF16)],
        compiler_params=_params("arbitrary", "arbitrary"),
        name="swiglu_up",
    )(x, wg, wu)


def _mm_res_ktiled_kernel(x_ref, w_ref, r_ref, o_ref):
    k = pl.program_id(2)

    @pl.when(k == 0)
    def _():
        o_ref[...] = r_ref[...] + _dot(x_ref[...], w_ref[...])

    @pl.when(k != 0)
    def _():
        o_ref[...] += _dot(x_ref[...], w_ref[...])


def _narrow_kernel(x_ref, w_ref, o_ref, wb_sc):
    @pl.when(pl.program_id(0) == 0)
    def _():
        wb_sc[...] = jnp.zeros(wb_sc.shape, BF16)
        wb_sc[0:w_ref.shape[0], :] = w_ref[...].astype(BF16)

    o_ref[...] = _dot_nt(x_ref[...], wb_sc[...])


def matmul_narrow(x, wt, layer, row0, nrows, bm=1024, name="proj_narrow"):
    m, k = x.shape
    bm = min(bm, m)
    assert m % bm == 0 and row0 % nrows == 0 and nrows % 16 == 0 and nrows <= LANES
    return pl.pallas_call(
        _narrow_kernel,
        grid=(m // bm,),
        in_specs=[pl.BlockSpec((bm, k), lambda i: (i, 0)),
                  pl.BlockSpec((None, nrows, k), lambda i: (layer, row0 // nrows, 0))],
        out_specs=pl.BlockSpec((bm, LANES), lambda i: (i, 0)),
        out_shape=jax.ShapeDtypeStruct((m, LANES), F32),
        scratch_shapes=[pltpu.VMEM((LANES, k), BF16)],
        compiler_params=_params("arbitrary"),
        name=name,
    )(x, wt)


def matmul_residual_ktiled(x, w, res, bk, bm=1024, bn=512, name="matmul_residual"):
    m, k = x.shape
    n = w.shape[1]
    bm, bn = min(bm, m), min(bn, n)
    assert m % bm == 0 and n % bn == 0 and k % bk == 0
    return pl.pallas_call(
        _mm_res_ktiled_kernel,
        grid=(m // bm, n // bn, k // bk),
        in_specs=[pl.BlockSpec((bm, bk), lambda i, j, kk: (i, kk)),
                  pl.BlockSpec((bk, bn), lambda i, j, kk: (kk, j)),
                  pl.BlockSpec((bm, bn), lambda i, j, kk: (i, j))],
        out_specs=pl.BlockSpec((bm, bn), lambda i, j, kk: (i, j)),
        out_shape=jax.ShapeDtypeStruct((m, n), F32),
        compiler_params=_params("parallel", "arbitrary", "arbitrary"),
        name=name,
    )(x, w, res)


def _attn_kernel(lam_ref, q_ref, k_ref, v_ref, w_ref, o_ref, m_sc, l_sc, acc_sc, *, blk, nsub, lam_init):
    d = ATT_HEAD_DIM
    nt = blk // LANES
    qi = pl.program_id(2)

    m_sc[...] = jnp.full(m_sc.shape, -jnp.inf, F32)
    l_sc[...] = jnp.zeros(l_sc.shape, F32)
    acc_sc[...] = jnp.zeros(acc_sc.shape, F32)

    row = lax.broadcasted_iota(jnp.int32, (blk, blk), 0)
    col = lax.broadcasted_iota(jnp.int32, (blk, blk), 1)
    keep = col <= row

    def chain(sub, c, k, v, masked):
        ch = sub * 2 + c
        q = q_ref[sub * blk:(sub + 1) * blk, c * d:(c + 1) * d]
        s = _dot_nt(q, k[:, c * d:(c + 1) * d])
        if masked:
            s = jnp.where(keep, s, -jnp.inf)
        tiles = [s[:, j * LANES:(j + 1) * LANES] for j in range(nt)]
        tile_max = functools.reduce(jnp.maximum, tiles)
        m_prev = m_sc[ch]
        m_new = jnp.maximum(m_prev, jnp.max(tile_max, axis=-1, keepdims=True))
        alpha = jnp.exp2(m_prev - m_new)
        ps = [jnp.exp2(tl - m_new) for tl in tiles]
        l_sc[ch] = alpha * l_sc[ch] + functools.reduce(jnp.add, ps)
        m_sc[ch] = m_new
        pv = _dot(jnp.concatenate(ps, axis=1).astype(BF16), v)
        for j in range(pv.shape[1] // LANES):
            sl = slice(j * LANES, (j + 1) * LANES)
            acc_sc[ch, :, sl] = alpha * acc_sc[ch, :, sl] + pv[:, sl]

    def step(kb, first_sub, diag_sub):
        start = pl.multiple_of(kb * blk, blk)
        k = k_ref[pl.ds(start, blk), :]
        v = v_ref[pl.ds(start, blk), :]
        for sub in range(first_sub, nsub):
            for c in range(2):
                chain(sub, c, k, v, sub == diag_sub)

    def body(kb, carry):
        step(kb, 0, None)
        return carry

    lax.fori_loop(0, qi * nsub, body, 0)
    for sub in range(nsub):
        step(qi * nsub + sub, sub, sub)

    lv = lam_ref[...]
    a1 = jnp.sum(lv[0:1] * lv[1:2], axis=-1, keepdims=True)
    a2 = jnp.sum(lv[2:3] * lv[3:4], axis=-1, keepdims=True)
    lam = jnp.exp(a1) - jnp.exp(a2) + lam_init
    gain = w_ref[...] * (1.0 - lam_init)
    for sub in range(nsub):
        l1 = jnp.sum(l_sc[2 * sub], axis=-1, keepdims=True)
        l2 = jnp.sum(l_sc[2 * sub + 1], axis=-1, keepdims=True)
        o = acc_sc[2 * sub] / l1 - lam * (acc_sc[2 * sub + 1] / l2)
        ms = jnp.mean(o * o, axis=-1, keepdims=True)
        o_ref[sub * blk:(sub + 1) * blk, :] = (o * lax.rsqrt(ms + EPS) * gain).astype(o_ref.dtype)


def diff_attention(qk, v, lam_vecs, subln_w, lam_init, batch, seq, blk=512, nsub=4):
    t, width = v.shape
    hv = 2 * ATT_HEAD_DIM
    heads = width // hv
    blk = min(blk, seq)
    nsub = min(nsub, seq // blk)
    bq = blk * nsub
    nq = seq // bq
    kern = functools.partial(_attn_kernel, blk=blk, nsub=nsub, lam_init=lam_init)
    return pl.pallas_call(
        kern,
        grid=(batch, heads, nq),
        in_specs=[pl.BlockSpec((4, ATT_HEAD_DIM), lambda b, h, i: (0, 0)),
                  pl.BlockSpec((bq, hv), lambda b, h, i: (b * nq + i, h)),
                  pl.BlockSpec((seq, hv), lambda b, h, i: (b, heads + h)),
                  pl.BlockSpec((seq, hv), lambda b, h, i: (b, h)),
                  pl.BlockSpec((1, hv), lambda b, h, i: (0, 0))],
        out_specs=pl.BlockSpec((bq, hv), lambda b, h, i: (b * nq + i, h)),
        out_shape=jax.ShapeDtypeStruct((t, width), BF16),
        scratch_shapes=[pltpu.VMEM((2 * nsub, blk, LANES), F32),
                        pltpu.VMEM((2 * nsub, blk, LANES), F32),
                        pltpu.VMEM((2 * nsub, blk, hv), F32)],
        compiler_params=_params("parallel", "parallel", "arbitrary"),
        name="diff_attention",
    )(lam_vecs, qk, qk, v, subln_w.reshape(1, hv))


def _ssd_kernel(xbc_ref, z_ref, dt_ref, cw_ref, cb_ref, dtb_ref, alog_ref, dsk_ref, nw_ref,
                o_ref, ext_sc, state_sc, *, width):
    L = SSM_CHUNK
    n = SSM_STATE
    gw = width // SSM_GROUPS
    hpg = gw // SSM_HEAD_DIM
    pad = 8
    c = pl.program_id(1)

    @pl.when(c == 0)
    def _():
        ext_sc[0:pad, :] = jnp.zeros((pad, ext_sc.shape[1]), F32)
        state_sc[...] = jnp.zeros(state_sc.shape, F32)

    ext_sc[pad:pad + L, :] = xbc_ref[...]

    def conv_silu(lo, hi):
        a = ext_sc[:, lo:hi]
        acc = cb_ref[:, lo:hi] + a[pad:pad + L] * cw_ref[CONV_K - 1:CONV_K, lo:hi]
        for tap in range(CONV_K - 1):
            shifted = pltpu.roll(a, CONV_K - 1 - tap, axis=0)[pad:pad + L]
            acc = acc + shifted * cw_ref[tap:tap + 1, lo:hi]
        return _silu(acc)

    dtr = dt_ref[...] + dtb_ref[...]
    dtp = jnp.maximum(dtr, 0.0) + jnp.log1p(jnp.exp(-jnp.abs(dtr)))
    a = -jnp.exp(alog_ref[...])
    d_a = dtp * a
    row = lax.broadcasted_iota(jnp.int32, (L, L), 0)
    col = lax.broadcasted_iota(jnp.int32, (L, L), 1)
    causal = col <= row
    tril = jnp.where(causal, 1.0, 0.0).astype(BF16)
    hi_, mid_, lo_ = _split3(d_a)
    acum = _dot(tril, hi_) + _dot(tril, mid_) + _dot(tril, lo_)
    acum_t = acum.T
    dtp_t = dtp.T
    e_acum = jnp.exp(acum)
    a_last = acum[L - 1:L, :]
    to_end = jnp.exp(a_last - acum) * dtp
    e_last = jnp.exp(a_last)

    head_of_col = lax.broadcasted_iota(jnp.int32, (1, gw), 1) // SSM_HEAD_DIM

    def expand(x, g):
        out = x[:, g * hpg:g * hpg + 1]
        for hd in range(1, hpg):
            out = jnp.where(head_of_col == hd, x[:, g * hpg + hd:g * hpg + hd + 1], out)
        return jnp.broadcast_to(out, (x.shape[0], gw))

    for g in range(SSM_GROUPS):
        xg = conv_silu(g * gw, (g + 1) * gw)
        bg = conv_silu(width + g * n, width + (g + 1) * n)
        cg = conv_silu(width + SSM_GROUPS * n + g * n, width + SSM_GROUPS * n + (g + 1) * n)
        xg16 = xg.astype(BF16)
        cg16 = cg.astype(BF16)
        cb = _dot_nt(cg16, bg.astype(BF16))
        y = None
        for hd in range(hpg):
            h = g * hpg + hd
            seg = acum[:, h:h + 1] - acum_t[h:h + 1, :]
            decay = jnp.exp(jnp.where(causal, seg, -jnp.inf))
            w = cb * decay * dtp_t[h:h + 1, :]
            yh = _dot(w.astype(BF16), xg16)
            y = yh if y is None else jnp.where(head_of_col == hd, yh, y)
        st = state_sc[g]
        y = y + _dot(cg16, st.astype(BF16)) * expand(e_acum, g)
        xw = (xg * expand(to_end, g)).astype(BF16)
        state_sc[g] = st * expand(e_last, g) + _dot(bg.T.astype(BF16), xw)
        y = y + dsk_ref[:, g * gw:(g + 1) * gw] * xg
        gated = y * _silu(z_ref[:, g * gw:(g + 1) * gw])
        ms = jnp.mean(gated * gated, axis=-1, keepdims=True)
        o_ref[:, g * gw:(g + 1) * gw] = (gated * lax.rsqrt(ms + EPS)
                                         * nw_ref[:, g * gw:(g + 1) * gw]).astype(o_ref.dtype)

    ext_sc[0:pad, :] = xbc_ref[L - pad:L, :]


def ssd_mixer(cz, dt, conv_w, conv_b, dt_bias, a_log, d_skip, norm_w, batch, seq):
    t = cz.shape[0]
    conv_ch = conv_w.shape[1]
    width = norm_w.shape[0]
    heads = width // SSM_HEAD_DIM
    L = SSM_CHUNK
    nc = seq // L
    assert cz.shape[1] == conv_ch + width and conv_ch % width == 0
    z_blk = conv_ch // width
    pad_heads = lambda v: jnp.pad(v.reshape(1, heads), ((0, 0), (0, LANES - heads)))
    dsk = jnp.repeat(d_skip, SSM_HEAD_DIM).reshape(1, width)
    return pl.pallas_call(
        functools.partial(_ssd_kernel, width=width),
        grid=(batch, nc),
        in_specs=[pl.BlockSpec((L, conv_ch), lambda b, c: (b * nc + c, 0)),
                  pl.BlockSpec((L, width), lambda b, c: (b * nc + c, z_blk)),
                  pl.BlockSpec((L, LANES), lambda b, c: (b * nc + c, 0)),
                  pl.BlockSpec((CONV_K, conv_ch), lambda b, c: (0, 0)),
                  pl.BlockSpec((1, conv_ch), lambda b, c: (0, 0)),
                  pl.BlockSpec((1, LANES), lambda b, c: (0, 0)),
                  pl.BlockSpec((1, LANES), lambda b, c: (0, 0)),
                  pl.BlockSpec((1, width), lambda b, c: (0, 0)),
                  pl.BlockSpec((1, width), lambda b, c: (0, 0))],
        out_specs=pl.BlockSpec((L, width), lambda b, c: (b * nc + c, 0)),
        out_shape=jax.ShapeDtypeStruct((t, width), BF16),
        scratch_shapes=[pltpu.VMEM((8 + L, conv_ch), F32),
                        pltpu.VMEM((SSM_GROUPS, SSM_STATE, width // SSM_GROUPS), F32)],
        compiler_params=_params("parallel", "arbitrary"),
        name="ssd_mixer",
    )(cz, cz, dt, conv_w, conv_b.reshape(1, conv_ch), pad_heads(dt_bias), pad_heads(a_log), dsk,
      norm_w.reshape(1, width))


def _norm_router_kernel(x_ref, w_ref, rw_ref, h_ref, idx_ref, gate_ref, *, n_experts):
    x = x_ref[...]
    ms = jnp.mean(x * x, axis=-1, keepdims=True)
    hn = x * lax.rsqrt(ms + EPS) * w_ref[...]
    h_ref[...] = hn
    hh, hm, _ = _split3(hn)
    rh, rm, _ = _split3(rw_ref[...])
    logits = _dot(hh, rh) + _dot(hh, rm) + _dot(hm, rh)
    lane = lax.broadcasted_iota(jnp.int32, logits.shape, 1)
    lanef = lane.astype(F32)
    logits = jnp.where(lane < n_experts, logits, -jnp.inf)
    m1 = jnp.max(logits, axis=-1, keepdims=True)
    i1 = jnp.min(jnp.where(logits == m1, lanef, float(LANES)), axis=-1, keepdims=True)
    rest = jnp.where(lanef == i1, -jnp.inf, logits)
    m2 = jnp.max(rest, axis=-1, keepdims=True)
    i2 = jnp.min(jnp.where(rest == m2, lanef, float(LANES)), axis=-1, keepdims=True)
    e = jnp.exp(m2 - m1)
    g1 = 1.0 / (1.0 + e)
    idx_ref[...] = jnp.where(lane == 0, i1, i2).astype(jnp.int32)
    gate_ref[...] = jnp.where(lane == 0, g1, e * g1)


def norm_router(x, w, router_w, bt=256):
    t, d = x.shape
    n_experts = router_w.shape[1]
    bt = min(bt, t)
    rw = jnp.pad(router_w, ((0, 0), (0, LANES - n_experts)))
    return pl.pallas_call(
        functools.partial(_norm_router_kernel, n_experts=n_experts),
        grid=(t // bt,),
        in_specs=[pl.BlockSpec((bt, d), lambda i: (i, 0)),
                  pl.BlockSpec((1, d), lambda i: (0, 0)),
                  pl.BlockSpec((d, LANES), lambda i: (0, 0))],
        out_specs=[pl.BlockSpec((bt, d), lambda i: (i, 0)),
                   pl.BlockSpec((bt, LANES), lambda i: (i, 0)),
                   pl.BlockSpec((bt, LANES), lambda i: (i, 0))],
        out_shape=[jax.ShapeDtypeStruct((t, d), F32),
                   jax.ShapeDtypeStruct((t, LANES), jnp.int32),
                   jax.ShapeDtypeStruct((t, LANES), F32)],
        compiler_params=_params("parallel"),
        name="norm_router",
    )(x, w.reshape(1, d), rw)


def _issue_row_gather(idx_ref, n_rows, src_hbm, dst_ref, sem):
    def issue(r, carry):
        pltpu.make_async_copy(src_hbm.at[pl.ds(idx_ref[0, 0, r], 1), :],
                              dst_ref.at[pl.ds(r, 1), :], sem).start()
        return carry

    lax.fori_loop(0, n_rows, issue, 0)


def _wait_row_gather(n_rows, src_hbm, dst_ref, sem):
    pltpu.make_async_copy(src_hbm.at[pl.ds(0, n_rows), :], dst_ref, sem).wait()


def _moe_gather_kernel(tv_ref, tok_ref, tok_next_ref, h_hbm, o_ref, rows_sc, sems):
    i = pl.program_id(0)
    n = pl.num_programs(0)
    bm = o_ref.shape[0]
    slot = lax.rem(i, 2)

    @pl.when(jnp.logical_and(i == 0, tv_ref[0] != 0))
    def _():
        _issue_row_gather(tok_ref, bm, h_hbm, rows_sc.at[0], sems.at[0])

    nxt = jnp.minimum(i + 1, n - 1)

    @pl.when(jnp.logical_and(i + 1 < n, tv_ref[nxt] != 0))
    def _():
        _issue_row_gather(tok_next_ref, bm, h_hbm, rows_sc.at[1 - slot], sems.at[1 - slot])

    @pl.when(tv_ref[i] != 0)
    def _():
        _wait_row_gather(bm, h_hbm, rows_sc.at[slot], sems.at[slot])
        o_ref[...] = rows_sc[slot].astype(o_ref.dtype)

    @pl.when(tv_ref[i] == 0)
    def _():
        o_ref[...] = jnp.zeros(o_ref.shape, o_ref.dtype)


def _expert_changed(te_ref, i):
    return jnp.logical_or(i == 0, te_ref[i] != te_ref[jnp.maximum(i - 1, 0)])


def _moe_up_kernel(te_ref, tv_ref, x_ref, wg_ref, wu_ref, o_ref, wgb_sc, wub_sc):
    i = pl.program_id(1)
    valid = tv_ref[i] != 0

    @pl.when(_expert_changed(te_ref, i))
    def _():
        wgb_sc[...] = wg_ref[...].astype(BF16)
        wub_sc[...] = wu_ref[...].astype(BF16)

    @pl.when(valid)
    def _():
        x = x_ref[...]
        g = _dot(x, wgb_sc[...])
        u = _dot(x, wub_sc[...])
        o_ref[...] = (_silu(g) * u).astype(o_ref.dtype)

    @pl.when(jnp.logical_not(valid))
    def _():
        o_ref[...] = jnp.zeros(o_ref.shape, o_ref.dtype)


def _moe_down_kernel(te_ref, tv_ref, h_ref, wd_ref, o_ref, wdb_sc):
    i = pl.program_id(1)
    valid = tv_ref[i] != 0

    @pl.when(_expert_changed(te_ref, i))
    def _():
        wdb_sc[...] = wd_ref[...].astype(BF16)

    @pl.when(valid)
    def _():
        o_ref[...] = _dot(h_ref[...], wdb_sc[...])

    @pl.when(jnp.logical_not(valid))
    def _():
        o_ref[...] = jnp.zeros(o_ref.shape, o_ref.dtype)


def _moe_combine_kernel(pos_ref, x_ref, g_ref, y_hbm, o_ref, rows_sc, sem):
    bt = x_ref.shape[0]
    _issue_row_gather(pos_ref, TOP_K * bt, y_hbm, rows_sc, sem)
    _wait_row_gather(TOP_K * bt, y_hbm, rows_sc, sem)
    g = g_ref[...]
    o_ref[...] = x_ref[...] + (g[:, 0:1] * rows_sc[0:bt, :] + g[:, 1:2] * rows_sc[bt:2 * bt, :])


def moe_ffn(xt, norm_w, router_w, w_gate, w_up, w_down, mi, bn=512, bt=256):
    t, d = xt.shape
    _, n_experts, _, f = w_gate.shape
    bm = min(MOE_TILE, t)
    bt = min(bt, t)
    h, idx, gates = norm_router(xt, norm_w, router_w)

    e_flat = idx[:, :TOP_K].reshape(-1)
    onehot = (e_flat[:, None] == jnp.arange(n_experts, dtype=jnp.int32)[None, :]).astype(jnp.int32)
    csum = jnp.cumsum(onehot, axis=0)
    rank = jnp.sum((csum - onehot) * onehot, axis=1)
    counts = csum[-1]
    tiles_e = (counts + bm - 1) // bm
    tile_end = jnp.cumsum(tiles_e)
    row_start = (tile_end - tiles_e) * bm
    pos = row_start[e_flat] + rank
    n_tiles = (TOP_K * t) // bm + n_experts
    n_rows = n_tiles * bm
    row_token = jnp.zeros((n_rows,), jnp.int32).at[pos].set(jnp.arange(TOP_K * t, dtype=jnp.int32) // TOP_K)
    tile_ids = jnp.arange(n_tiles, dtype=jnp.int32)
    n_valid = tile_end[-1]
    tile_valid = (tile_ids < n_valid).astype(jnp.int32)
    last_valid = jnp.minimum(tile_ids, n_valid - 1)
    tile_expert = jnp.sum((last_valid[:, None] >= tile_end[None, :]).astype(jnp.int32), axis=1)
    tile_expert = jnp.minimum(tile_expert, n_experts - 1)

    tok_tiles = row_token.reshape(n_tiles, 1, bm)
    xs = pl.pallas_call(
        _moe_gather_kernel,
        grid_spec=pltpu.PrefetchScalarGridSpec(
            num_scalar_prefetch=1,
            grid=(n_tiles,),
            in_specs=[pl.BlockSpec((1, 1, bm), lambda i, tv: (i, 0, 0), memory_space=pltpu.SMEM),
                      pl.BlockSpec((1, 1, bm), lambda i, tv: (jnp.minimum(i + 1, n_tiles - 1), 0, 0),
                                   memory_space=pltpu.SMEM),
                      pl.BlockSpec(memory_space=pl.ANY)],
            out_specs=pl.BlockSpec((bm, d), lambda i, tv: (i, 0)),
            scratch_shapes=[pltpu.VMEM((2, bm, d), F32), pltpu.SemaphoreType.DMA((2,))]),
        out_shape=jax.ShapeDtypeStruct((n_rows, d), BF16),
        compiler_params=_params("arbitrary"),
        name="moe_gather",
    )(tile_valid, tok_tiles, tok_tiles, h)

    hid = pl.pallas_call(
        _moe_up_kernel,
        grid_spec=pltpu.PrefetchScalarGridSpec(
            num_scalar_prefetch=2,
            grid=(f // bn, n_tiles),
            in_specs=[pl.BlockSpec((bm, d), lambda j, i, te, tv: (i, 0)),
                      pl.BlockSpec((None, None, d, bn), lambda j, i, te, tv: (mi, te[i], 0, j)),
                      pl.BlockSpec((None, None, d, bn), lambda j, i, te, tv: (mi, te[i], 0, j))],
            out_specs=pl.BlockSpec((bm, bn), lambda j, i, te, tv: (i, j)),
            scratch_shapes=[pltpu.VMEM((d, bn), BF16), pltpu.VMEM((d, bn), BF16)]),
        out_shape=jax.ShapeDtypeStruct((n_rows, f), BF16),
        compiler_params=_params("arbitrary", "arbitrary", vmem=VMEM_LIMIT_BIG),
        name="moe_up",
    )(tile_expert, tile_valid, xs, w_gate, w_up)

    y = pl.pallas_call(
        _moe_down_kernel,
        grid_spec=pltpu.PrefetchScalarGridSpec(
            num_scalar_prefetch=2,
            grid=(d // bn, n_tiles),
            in_specs=[pl.BlockSpec((bm, f), lambda j, i, te, tv: (i, 0)),
                      pl.BlockSpec((None, None, f, bn), lambda j, i, te, tv: (mi, te[i], 0, j))],
            out_specs=pl.BlockSpec((bm, bn), lambda j, i, te, tv: (i, j)),
            scratch_shapes=[pltpu.VMEM((f, bn), BF16)]),
        out_shape=jax.ShapeDtypeStruct((n_rows, d), F32),
        compiler_params=_params("arbitrary", "arbitrary"),
        name="moe_down",
    )(tile_expert, tile_valid, hid, w_down)

    pos_tiles = pos.reshape(t // bt, bt, TOP_K).transpose(0, 2, 1).reshape(t // bt, 1, TOP_K * bt)
    return pl.pallas_call(
        _moe_combine_kernel,
        grid=(t // bt,),
        in_specs=[pl.BlockSpec((1, 1, TOP_K * bt), lambda i: (i, 0, 0), memory_space=pltpu.SMEM),
                  pl.BlockSpec((bt, d), lambda i: (i, 0)),
                  pl.BlockSpec((bt, LANES), lambda i: (i, 0)),
                  pl.BlockSpec(memory_space=pl.ANY)],
        out_specs=pl.BlockSpec((bt, d), lambda i: (i, 0)),
        out_shape=jax.ShapeDtypeStruct((t, d), F32),
        scratch_shapes=[pltpu.VMEM((TOP_K * bt, d), F32), pltpu.SemaphoreType.DMA(())],
        compiler_params=_params("arbitrary"),
        name="moe_combine",
    )(pos_tiles, xt, gates, y)


def _lambda_init(layer):
    return 0.8 - 0.6 * math.exp(-0.3 * layer)


def kernel(x, norm1_w, w_in, w_out, q_norm_w, k_norm_w, lambda_q1, lambda_k1, lambda_q2, lambda_k2,
           subln_w, conv_w, conv_b, dt_bias, a_log, d_skip, ssm_norm_w, norm2_w, ffn_w_gate, ffn_w_up,
           ffn_w_down, router_w, moe_w_gate, moe_w_up, moe_w_down):
    b, s, d = x.shape
    t = b * s
    depth = w_in.shape[0]
    ssm_width = ssm_norm_w.shape[1]
    conv_ch = conv_w.shape[2]
    ssm_heads = dt_bias.shape[1]
    att_width = (w_in.shape[2] - ssm_width - conv_ch - ssm_heads) // 3
    n_qk_heads = att_width // ATT_HEAD_DIM
    q_scale = ATT_HEAD_DIM ** -0.5 * math.log2(math.e)
    bn = 512
    z0 = 3 * att_width
    c0 = z0 + ssm_width
    nz, nx = ssm_width // bn, conv_ch // bn

    xt = x.reshape(t, d)
    w_in_t = jnp.swapaxes(w_in, 1, 2)
    for layer in range(depth):

        h = rmsnorm(xt, norm1_w[layer])
        gain = jnp.concatenate([jnp.tile(q_norm_w[layer] * q_scale, n_qk_heads),
                                jnp.tile(k_norm_w[layer], n_qk_heads)]).reshape(1, 2 * att_width)
        qk = matmul_ws(h, w_in_t, layer, 0, 2 * att_width, BF16, gain=gain, bn=bn, name="proj_qk")
        v = matmul_ws(h, w_in_t, layer, 2 * att_width, att_width, BF16, bn=bn, name="proj_v")
        cz = matmul_ws(h, w_in_t, layer, z0, ssm_width + conv_ch, F32, bn=bn, name="proj_ssm",
                       out_block=lambda j: jnp.where(j < nz, j + nx, j - nz))
        dt = matmul_narrow(h, w_in_t, layer, c0 + conv_ch, ssm_heads, name="proj_dt")

        lam_vecs = jnp.stack([lambda_q1[layer], lambda_k1[layer], lambda_q2[layer], lambda_k2[layer]])
        att = diff_attention(qk, v, lam_vecs, subln_w[layer], _lambda_init(layer), b, s)
        ssm = ssd_mixer(cz, dt, conv_w[layer], conv_b[layer], dt_bias[layer], a_log[layer],
                        d_skip[layer], ssm_norm_w[layer], b, s)
        xt = matmul_ws_res2(att, ssm, w_out, layer, xt, bn=bn)

        i = layer // 2
        if layer % 2 == 0:
            f = ffn_w_gate.shape[2]
            h2 = rmsnorm(xt, norm2_w[layer])
            hid = swiglu_up_ws(h2, ffn_w_gate, ffn_w_up, i)
            xt = matmul_residual_ktiled(hid, ffn_w_down[i].astype(BF16), xt, bk=f // 2, name="ffn_down")
        else:
            xt = moe_ffn(xt, norm2_w[layer], router_w[i], moe_w_gate, moe_w_up, moe_w_down, i)
    return xt.reshape(b, s, d)
```

```python
import functools
import math

import jax
import jax.numpy as jnp
from jax import lax
from jax.experimental import pallas as pl
from jax.experimental.pallas import tpu as pltpu

F32 = jnp.float32
BF16 = jnp.bfloat16
EPS = 1e-6
LANES = 128
VMEM_LIMIT = 56 * 1024 * 1024
VMEM_LIMIT_BIG = 60 * 1024 * 1024

ATT_HEAD_DIM = 128
SSM_HEAD_DIM = 64
SSM_GROUPS = 8
SSM_STATE = 128
CONV_K = 4
SSM_CHUNK = 128
TOP_K = 2
MOE_TILE = 512


def _params(*sem, vmem=VMEM_LIMIT):
    return pltpu.CompilerParams(dimension_semantics=sem, vmem_limit_bytes=vmem)


def _dot(a, b):
    return jnp.dot(a, b, preferred_element_type=F32)


def _dot_nt(a, b):
    return lax.dot_general(a, b, (((1,), (1,)), ((), ())), preferred_element_type=F32)


def _silu(x):
    return x / (1.0 + jnp.exp(-x))


def _split3(x):
    hi = x.astype(BF16)
    r1 = x - hi.astype(F32)
    mid = r1.astype(BF16)
    lo = (r1 - mid.astype(F32)).astype(BF16)
    return hi, mid, lo


def _rmsnorm_kernel(x_ref, w_ref, o_ref):
    x = x_ref[...]
    ms = jnp.mean(x * x, axis=-1, keepdims=True)
    o_ref[...] = (x * lax.rsqrt(ms + EPS) * w_ref[...]).astype(o_ref.dtype)


def rmsnorm(x, w, out_dtype=BF16, bt=256):
    t, d = x.shape
    bt = min(bt, t)
    return pl.pallas_call(
        _rmsnorm_kernel,
        grid=(t // bt,),
        in_specs=[pl.BlockSpec((bt, d), lambda i: (i, 0)),
                  pl.BlockSpec((1, d), lambda i: (0, 0))],
        out_specs=pl.BlockSpec((bt, d), lambda i: (i, 0)),
        out_shape=jax.ShapeDtypeStruct((t, d), out_dtype),
        compiler_params=_params("parallel"),
        name="rmsnorm",
    )(x, w.reshape(1, d))


def _cast_weight(w_ref, wb_sc):
    @pl.when(pl.program_id(1) == 0)
    def _():
        wb_sc[...] = w_ref[...].astype(BF16)


def _ws_kernel(x_ref, w_ref, o_ref, wb_sc, *, dot):
    _cast_weight(w_ref, wb_sc)
    o_ref[...] = dot(x_ref[...], wb_sc[...]).astype(o_ref.dtype)


def _ws_headnorm_kernel(x_ref, w_ref, g_ref, o_ref, wb_sc, *, dot):
    _cast_weight(w_ref, wb_sc)
    acc = dot(x_ref[...], wb_sc[...])
    for c in range(acc.shape[1] // ATT_HEAD_DIM):
        sl = slice(c * ATT_HEAD_DIM, (c + 1) * ATT_HEAD_DIM)
        a = acc[:, sl]
        ms = jnp.mean(a * a, axis=-1, keepdims=True)
        o_ref[:, sl] = (a * lax.rsqrt(ms + EPS) * g_ref[:, sl]).astype(o_ref.dtype)


def _ws_res2_kernel(x1_ref, x2_ref, w_ref, r_ref, o_ref, wb_sc):
    _cast_weight(w_ref, wb_sc)
    k1 = x1_ref.shape[1]
    o_ref[...] = (r_ref[...] + _dot(x1_ref[...], wb_sc[0:k1, :])) + _dot(x2_ref[...], wb_sc[k1:, :])


def _ws_swiglu_kernel(x_ref, wg_ref, wu_ref, o_ref, wgb_sc, wub_sc):
    _cast_weight(wg_ref, wgb_sc)
    _cast_weight(wu_ref, wub_sc)
    x = x_ref[...]
    g = _dot(x, wgb_sc[...])
    u = _dot(x, wub_sc[...])
    o_ref[...] = (_silu(g) * u).astype(o_ref.dtype)


def matmul_ws(x, wt, layer, col0, ncols, out_dtype, gain=None, out_block=None, bm=1024, bn=512, name="proj"):
    m, k = x.shape
    bm = min(bm, m)
    assert m % bm == 0 and ncols % bn == 0 and col0 % bn == 0
    j0 = col0 // bn
    ob = out_block if out_block is not None else (lambda j: j)
    in_specs = [pl.BlockSpec((bm, k), lambda j, i: (i, 0)),
                pl.BlockSpec((None, bn, k), lambda j, i: (layer, j0 + j, 0))]
    args = [x, wt]
    kern = _ws_kernel
    if gain is not None:
        in_specs.append(pl.BlockSpec((1, bn), lambda j, i: (0, j)))
        args.append(gain)
        kern = _ws_headnorm_kernel
    return pl.pallas_call(
        functools.partial(kern, dot=_dot_nt),
        grid=(ncols // bn, m // bm),
        in_specs=in_specs,
        out_specs=pl.BlockSpec((bm, bn), lambda j, i: (i, ob(j))),
        out_shape=jax.ShapeDtypeStruct((m, ncols), out_dtype),
        scratch_shapes=[pltpu.VMEM((bn, k), BF16)],
        compiler_params=_params("arbitrary", "arbitrary"),
        name=name,
    )(*args)


def matmul_ws_res2(x1, x2, w, layer, res, bm=1024, bn=512, name="out_proj"):
    m, k1 = x1.shape
    k2 = x2.shape[1]
    n = w.shape[2]
    bm = min(bm, m)
    assert m % bm == 0 and n % bn == 0 and w.shape[1] == k1 + k2
    return pl.pallas_call(
        _ws_res2_kernel,
        grid=(n // bn, m // bm),
        in_specs=[pl.BlockSpec((bm, k1), lambda j, i: (i, 0)),
                  pl.BlockSpec((bm, k2), lambda j, i: (i, 0)),
                  pl.BlockSpec((None, k1 + k2, bn), lambda j, i: (layer, 0, j)),
                  pl.BlockSpec((bm, bn), lambda j, i: (i, j))],
        out_specs=pl.BlockSpec((bm, bn), lambda j, i: (i, j)),
        out_shape=jax.ShapeDtypeStruct((m, n), F32),
        scratch_shapes=[pltpu.VMEM((k1 + k2, bn), BF16)],
        compiler_params=_params("arbitrary", "arbitrary"),
        name=name,
    )(x1, x2, w, res)


def swiglu_up_ws(x, wg, wu, layer, bm=1024, bn=256):
    m, k = x.shape
    n = wg.shape[2]
    bm = min(bm, m)
    assert m % bm == 0 and n % bn == 0
    return pl.pallas_call(
        _ws_swiglu_kernel,
        grid=(n // bn, m // bm),
        in_specs=[pl.BlockSpec((bm, k), lambda j, i: (i, 0)),
                  pl.BlockSpec((None, k, bn), lambda j, i: (layer, 0, j)),
                  pl.BlockSpec((None, k, bn), lambda j, i: (layer, 0, j))],
        out_specs=pl.BlockSpec((bm, bn), lambda j, i: (i, j)),
        out_shape=jax.ShapeDtypeStruct((m, n), BF16),
        scratch_shapes=[pltpu.VMEM((k, bn), BF16), pltpu.VMEM((k, bn), BF16)],
        compiler_params=_params("arbitrary", "arbitrary"),
        name="swiglu_up",
    )(x, wg, wu)


def _mm_res_ktiled_kernel(x_ref, w_ref, r_ref, o_ref):
    k = pl.program_id(2)

    @pl.when(k == 0)
    def _():
        o_ref[...] = r_ref[...] + _dot(x_ref[...], w_ref[...])

    @pl.when(k != 0)
    def _():
        o_ref[...] += _dot(x_ref[...], w_ref[...])


def _narrow_kernel(x_ref, w_ref, o_ref, wb_sc):
    @pl.when(pl.program_id(0) == 0)
    def _():
        wb_sc[...] = jnp.zeros(wb_sc.shape, BF16)
        wb_sc[0:w_ref.shape[0], :] = w_ref[...].astype(BF16)

    o_ref[...] = _dot_nt(x_ref[...], wb_sc[...])


def matmul_narrow(x, wt, layer, row0, nrows, bm=1024, name="proj_narrow"):
    m, k = x.shape
    bm = min(bm, m)
    assert m % bm == 0 and row0 % nrows == 0 and nrows % 16 == 0 and nrows <= LANES
    return pl.pallas_call(
        _narrow_kernel,
        grid=(m // bm,),
        in_specs=[pl.BlockSpec((bm, k), lambda i: (i, 0)),
                  pl.BlockSpec((None, nrows, k), lambda i: (layer, row0 // nrows, 0))],
        out_specs=pl.BlockSpec((bm, LANES), lambda i: (i, 0)),
        out_shape=jax.ShapeDtypeStruct((m, LANES), F32),
        scratch_shapes=[pltpu.VMEM((LANES, k), BF16)],
        compiler_params=_params("arbitrary"),
        name=name,
    )(x, wt)


def matmul_residual_ktiled(x, w, res, bk, bm=1024, bn=512, name="matmul_residual"):
    m, k = x.shape
    n = w.shape[1]
    bm, bn = min(bm, m), min(bn, n)
    assert m % bm == 0 and n % bn == 0 and k % bk == 0
    return pl.pallas_call(
        _mm_res_ktiled_kernel,
        grid=(m // bm, n // bn, k // bk),
        in_specs=[pl.BlockSpec((bm, bk), lambda i, j, kk: (i, kk)),
                  pl.BlockSpec((bk, bn), lambda i, j, kk: (kk, j)),
                  pl.BlockSpec((bm, bn), lambda i, j, kk: (i, j))],
        out_specs=pl.BlockSpec((bm, bn), lambda i, j, kk: (i, j)),
        out_shape=jax.ShapeDtypeStruct((m, n), F32),
        compiler_params=_params("parallel", "arbitrary", "arbitrary"),
        name=name,
    )(x, w, res)


def _attn_kernel(lam_ref, q_ref, k_ref, v_ref, w_ref, o_ref, m_sc, l_sc, acc_sc, *, blk, nsub, lam_init):
    d = ATT_HEAD_DIM
    nt = blk // LANES
    qi = pl.program_id(2)

    m_sc[...] = jnp.full(m_sc.shape, -jnp.inf, F32)
    l_sc[...] = jnp.zeros(l_sc.shape, F32)
    acc_sc[...] = jnp.zeros(acc_sc.shape, F32)

    row = lax.broadcasted_iota(jnp.int32, (blk, blk), 0)
    col = lax.broadcasted_iota(jnp.int32, (blk, blk), 1)
    keep = col <= row

    def chain(sub, c, k, v, masked):
        ch = sub * 2 + c
        q = q_ref[sub * blk:(sub + 1) * blk, c * d:(c + 1) * d]
        s = _dot_nt(q, k[:, c * d:(c + 1) * d])
        if masked:
            s = jnp.where(keep, s, -jnp.inf)
        tiles = [s[:, j * LANES:(j + 1) * LANES] for j in range(nt)]
        tile_max = functools.reduce(jnp.maximum, tiles)
        m_prev = m_sc[ch]
        m_new = jnp.maximum(m_prev, jnp.max(tile_max, axis=-1, keepdims=True))
        alpha = jnp.exp2(m_prev - m_new)
        ps = [jnp.exp2(tl - m_new) for tl in tiles]
        l_sc[ch] = alpha * l_sc[ch] + functools.reduce(jnp.add, ps)
        m_sc[ch] = m_new
        pv = _dot(jnp.concatenate(ps, axis=1).astype(BF16), v)
        for j in range(pv.shape[1] // LANES):
            sl = slice(j * LANES, (j + 1) * LANES)
            acc_sc[ch, :, sl] = alpha * acc_sc[ch, :, sl] + pv[:, sl]

    def step(kb, first_sub, diag_sub):
        start = pl.multiple_of(kb * blk, blk)
        k = k_ref[pl.ds(start, blk), :]
        v = v_ref[pl.ds(start, blk), :]
        for sub in range(first_sub, nsub):
            for c in range(2):
                chain(sub, c, k, v, sub == diag_sub)

    def body(kb, carry):
        step(kb, 0, None)
        return carry

    lax.fori_loop(0, qi * nsub, body, 0)
    for sub in range(nsub):
        step(qi * nsub + sub, sub, sub)

    lv = lam_ref[...]
    a1 = jnp.sum(lv[0:1] * lv[1:2], axis=-1, keepdims=True)
    a2 = jnp.sum(lv[2:3] * lv[3:4], axis=-1, keepdims=True)
    lam = jnp.exp(a1) - jnp.exp(a2) + lam_init
    gain = w_ref[...] * (1.0 - lam_init)
    for sub in range(nsub):
        l1 = jnp.sum(l_sc[2 * sub], axis=-1, keepdims=True)
        l2 = jnp.sum(l_sc[2 * sub + 1], axis=-1, keepdims=True)
        o = acc_sc[2 * sub] / l1 - lam * (acc_sc[2 * sub + 1] / l2)
        ms = jnp.mean(o * o, axis=-1, keepdims=True)
        o_ref[sub * blk:(sub + 1) * blk, :] = (o * lax.rsqrt(ms + EPS) * gain).astype(o_ref.dtype)


def diff_attention(qk, v, lam_vecs, subln_w, lam_init, batch, seq, blk=512, nsub=4):
    t, width = v.shape
    hv = 2 * ATT_HEAD_DIM
    heads = width // hv
    blk = min(blk, seq)
    nsub = min(nsub, seq // blk)
    bq = blk * nsub
    nq = seq // bq
    kern = functools.partial(_attn_kernel, blk=blk, nsub=nsub, lam_init=lam_init)
    return pl.pallas_call(
        kern,
        grid=(batch, heads, nq),
        in_specs=[pl.BlockSpec((4, ATT_HEAD_DIM), lambda b, h, i: (0, 0)),
                  pl.BlockSpec((bq, hv), lambda b, h, i: (b * nq + i, h)),
                  pl.BlockSpec((seq, hv), lambda b, h, i: (b, heads + h)),
                  pl.BlockSpec((seq, hv), lambda b, h, i: (b, h)),
                  pl.BlockSpec((1, hv), lambda b, h, i: (0, 0))],
        out_specs=pl.BlockSpec((bq, hv), lambda b, h, i: (b * nq + i, h)),
        out_shape=jax.ShapeDtypeStruct((t, width), BF16),
        scratch_shapes=[pltpu.VMEM((2 * nsub, blk, LANES), F32),
                        pltpu.VMEM((2 * nsub, blk, LANES), F32),
                        pltpu.VMEM((2 * nsub, blk, hv), F32)],
        compiler_params=_params("parallel", "parallel", "arbitrary"),
        name="diff_attention",
    )(lam_vecs, qk, qk, v, subln_w.reshape(1, hv))


def _ssd_kernel(xbc_ref, z_ref, dt_ref, cw_ref, cb_ref, dtb_ref, alog_ref, dsk_ref, nw_ref,
                o_ref, ext_sc, state_sc, *, width):
    L = SSM_CHUNK
    n = SSM_STATE
    gw = width // SSM_GROUPS
    hpg = gw // SSM_HEAD_DIM
    pad = 8
    c = pl.program_id(1)

    @pl.when(c == 0)
    def _():
        ext_sc[0:pad, :] = jnp.zeros((pad, ext_sc.shape[1]), F32)
        state_sc[...] = jnp.zeros(state_sc.shape, F32)

    ext_sc[pad:pad + L, :] = xbc_ref[...]

    def conv_silu(lo, hi):
        a = ext_sc[:, lo:hi]
        acc = cb_ref[:, lo:hi] + a[pad:pad + L] * cw_ref[CONV_K - 1:CONV_K, lo:hi]
        for tap in range(CONV_K - 1):
            shifted = pltpu.roll(a, CONV_K - 1 - tap, axis=0)[pad:pad + L]
            acc = acc + shifted * cw_ref[tap:tap + 1, lo:hi]
        return _silu(acc)

    dtr = dt_ref[...] + dtb_ref[...]
    dtp = jnp.maximum(dtr, 0.0) + jnp.log1p(jnp.exp(-jnp.abs(dtr)))
    a = -jnp.exp(alog_ref[...])
    d_a = dtp * a
    row = lax.broadcasted_iota(jnp.int32, (L, L), 0)
    col = lax.broadcasted_iota(jnp.int32, (L, L), 1)
    causal = col <= row
    tril = jnp.where(causal, 1.0, 0.0).astype(BF16)
    hi_, mid_, lo_ = _split3(d_a)
    acum = _dot(tril, hi_) + _dot(tril, mid_) + _dot(tril, lo_)
    acum_t = acum.T
    dtp_t = dtp.T
    e_acum = jnp.exp(acum)
    a_last = acum[L - 1:L, :]
    to_end = jnp.exp(a_last - acum) * dtp
    e_last = jnp.exp(a_last)

    head_of_col = lax.broadcasted_iota(jnp.int32, (1, gw), 1) // SSM_HEAD_DIM

    def expand(x, g):
        out = x[:, g * hpg:g * hpg + 1]
        for hd in range(1, hpg):
            out = jnp.where(head_of_col == hd, x[:, g * hpg + hd:g * hpg + hd + 1], out)
        return jnp.broadcast_to(out, (x.shape[0], gw))

    for g in range(SSM_GROUPS):
        xg = conv_silu(g * gw, (g + 1) * gw)
        bg = conv_silu(width + g * n, width + (g + 1) * n)
        cg = conv_silu(width + SSM_GROUPS * n + g * n, width + SSM_GROUPS * n + (g + 1) * n)
        xg16 = xg.astype(BF16)
        cg16 = cg.astype(BF16)
        cb = _dot_nt(cg16, bg.astype(BF16))
        y = None
        for hd in range(hpg):
            h = g * hpg + hd
            seg = acum[:, h:h + 1] - acum_t[h:h + 1, :]
            decay = jnp.exp(jnp.where(causal, seg, -jnp.inf))
            w = cb * decay * dtp_t[h:h + 1, :]
            yh = _dot(w.astype(BF16), xg16)
            y = yh if y is None else jnp.where(head_of_col == hd, yh, y)
        st = state_sc[g]
        y = y + _dot(cg16, st.astype(BF16)) * expand(e_acum, g)
        xw = (xg * expand(to_end, g)).astype(BF16)
        state_sc[g] = st * expand(e_last, g) + _dot(bg.T.astype(BF16), xw)
        y = y + dsk_ref[:, g * gw:(g + 1) * gw] * xg
        gated = y * _silu(z_ref[:, g * gw:(g + 1) * gw])
        ms = jnp.mean(gated * gated, axis=-1, keepdims=True)
        o_ref[:, g * gw:(g + 1) * gw] = (gated * lax.rsqrt(ms + EPS)
                                         * nw_ref[:, g * gw:(g + 1) * gw]).astype(o_ref.dtype)

    ext_sc[0:pad, :] = xbc_ref[L - pad:L, :]


def ssd_mixer(cz, dt, conv_w, conv_b, dt_bias, a_log, d_skip, norm_w, batch, seq):
    t = cz.shape[0]
    conv_ch = conv_w.shape[1]
    width = norm_w.shape[0]
    heads = width // SSM_HEAD_DIM
    L = SSM_CHUNK
    nc = seq // L
    assert cz.shape[1] == conv_ch + width and conv_ch % width == 0
    z_blk = conv_ch // width
    pad_heads = lambda v: jnp.pad(v.reshape(1, heads), ((0, 0), (0, LANES - heads)))
    dsk = jnp.repeat(d_skip, SSM_HEAD_DIM).reshape(1, width)
    return pl.pallas_call(
        functools.partial(_ssd_kernel, width=width),
        grid=(batch, nc),
        in_specs=[pl.BlockSpec((L, conv_ch), lambda b, c: (b * nc + c, 0)),
                  pl.BlockSpec((L, width), lambda b, c: (b * nc + c, z_blk)),
                  pl.BlockSpec((L, LANES), lambda b, c: (b * nc + c, 0)),
                  pl.BlockSpec((CONV_K, conv_ch), lambda b, c: (0, 0)),
                  pl.BlockSpec((1, conv_ch), lambda b, c: (0, 0)),
                  pl.BlockSpec((1, LANES), lambda b, c: (0, 0)),
                  pl.BlockSpec((1, LANES), lambda b, c: (0, 0)),
                  pl.BlockSpec((1, width), lambda b, c: (0, 0)),
                  pl.BlockSpec((1, width), lambda b, c: (0, 0))],
        out_specs=pl.BlockSpec((L, width), lambda b, c: (b * nc + c, 0)),
        out_shape=jax.ShapeDtypeStruct((t, width), BF16),
        scratch_shapes=[pltpu.VMEM((8 + L, conv_ch), F32),
                        pltpu.VMEM((SSM_GROUPS, SSM_STATE, width // SSM_GROUPS), F32)],
        compiler_params=_params("parallel", "arbitrary"),
        name="ssd_mixer",
    )(cz, cz, dt, conv_w, conv_b.reshape(1, conv_ch), pad_heads(dt_bias), pad_heads(a_log), dsk,
      norm_w.reshape(1, width))


def _norm_router_kernel(x_ref, w_ref, rw_ref, h_ref, idx_ref, gate_ref, *, n_experts):
    x = x_ref[...]
    ms = jnp.mean(x * x, axis=-1, keepdims=True)
    hn = x * lax.rsqrt(ms + EPS) * w_ref[...]
    h_ref[...] = hn
    hh, hm, _ = _split3(hn)
    rh, rm, _ = _split3(rw_ref[...])
    logits = _dot(hh, rh) + _dot(hh, rm) + _dot(hm, rh)
    lane = lax.broadcasted_iota(jnp.int32, logits.shape, 1)
    lanef = lane.astype(F32)
    logits = jnp.where(lane < n_experts, logits, -jnp.inf)
    m1 = jnp.max(logits, axis=-1, keepdims=True)
    i1 = jnp.min(jnp.where(logits == m1, lanef, float(LANES)), axis=-1, keepdims=True)
    rest = jnp.where(lanef == i1, -jnp.inf, logits)
    m2 = jnp.max(rest, axis=-1, keepdims=True)
    i2 = jnp.min(jnp.where(rest == m2, lanef, float(LANES)), axis=-1, keepdims=True)
    e = jnp.exp(m2 - m1)
    g1 = 1.0 / (1.0 + e)
    idx_ref[...] = jnp.where(lane == 0, i1, i2).astype(jnp.int32)
    gate_ref[...] = jnp.where(lane == 0, g1, e * g1)


def norm_router(x, w, router_w, bt=256):
    t, d = x.shape
    n_experts = router_w.shape[1]
    bt = min(bt, t)
    rw = jnp.pad(router_w, ((0, 0), (0, LANES - n_experts)))
    return pl.pallas_call(
        functools.partial(_norm_router_kernel, n_experts=n_experts),
        grid=(t // bt,),
        in_specs=[pl.BlockSpec((bt, d), lambda i: (i, 0)),
                  pl.BlockSpec((1, d), lambda i: (0, 0)),
                  pl.BlockSpec((d, LANES), lambda i: (0, 0))],
        out_specs=[pl.BlockSpec((bt, d), lambda i: (i, 0)),
                   pl.BlockSpec((bt, LANES), lambda i: (i, 0)),
                   pl.BlockSpec((bt, LANES), lambda i: (i, 0))],
        out_shape=[jax.ShapeDtypeStruct((t, d), F32),
                   jax.ShapeDtypeStruct((t, LANES), jnp.int32),
                   jax.ShapeDtypeStruct((t, LANES), F32)],
        compiler_params=_params("parallel"),
        name="norm_router",
    )(x, w.reshape(1, d), rw)


def _issue_row_gather(idx_ref, n_rows, src_hbm, dst_ref, sem):
    def issue(r, carry):
        pltpu.make_async_copy(src_hbm.at[pl.ds(idx_ref[0, 0, r], 1), :],
                              dst_ref.at[pl.ds(r, 1), :], sem).start()
        return carry

    lax.fori_loop(0, n_rows, issue, 0)


def _wait_row_gather(n_rows, src_hbm, dst_ref, sem):
    pltpu.make_async_copy(src_hbm.at[pl.ds(0, n_rows), :], dst_ref, sem).wait()


def _moe_gather_kernel(tv_ref, tok_ref, tok_next_ref, h_hbm, o_ref, rows_sc, sems):
    i = pl.program_id(0)
    n = pl.num_programs(0)
    bm = o_ref.shape[0]
    slot = lax.rem(i, 2)

    @pl.when(jnp.logical_and(i == 0, tv_ref[0] != 0))
    def _():
        _issue_row_gather(tok_ref, bm, h_hbm, rows_sc.at[0], sems.at[0])

    nxt = jnp.minimum(i + 1, n - 1)

    @pl.when(jnp.logical_and(i + 1 < n, tv_ref[nxt] != 0))
    def _():
        _issue_row_gather(tok_next_ref, bm, h_hbm, rows_sc.at[1 - slot], sems.at[1 - slot])

    @pl.when(tv_ref[i] != 0)
    def _():
        _wait_row_gather(bm, h_hbm, rows_sc.at[slot], sems.at[slot])
        o_ref[...] = rows_sc[slot].astype(o_ref.dtype)

    @pl.when(tv_ref[i] == 0)
    def _():
        o_ref[...] = jnp.zeros(o_ref.shape, o_ref.dtype)


def _expert_weights(te_ref, ci_ref, ne_ref, nc_ref, mi, bn, w_hbms, wf_scs, wb_scs, sems):
    j = pl.program_id(0)
    i = pl.program_id(1)
    nj = pl.num_programs(0)
    changed = jnp.logical_or(i == 0, te_ref[i] != te_ref[jnp.maximum(i - 1, 0)])

    def copies(e, jj, slot):
        col = pl.multiple_of(jj * bn, bn)
        return [pltpu.make_async_copy(w.at[mi, e, :, pl.ds(col, bn)], wf.at[slot], sem.at[slot])
                for w, wf, sem in zip(w_hbms, wf_scs, sems)]

    @pl.when(changed)
    def _():
        c = j * nc_ref[0] + ci_ref[i]
        slot = lax.rem(c, 2)

        @pl.when(c == 0)
        def _():
            for cp in copies(te_ref[0], 0, 0):
                cp.start()

        for cp in copies(te_ref[i], j, slot):
            cp.wait()
        for wf, wb in zip(wf_scs, wb_scs):
            wb[...] = wf[slot].astype(BF16)

        more_here = ne_ref[i] >= 0

        @pl.when(more_here)
        def _():
            for cp in copies(ne_ref[i], j, 1 - slot):
                cp.start()

        @pl.when(jnp.logical_and(jnp.logical_not(more_here), j + 1 < nj))
        def _():
            for cp in copies(te_ref[0], j + 1, 1 - slot):
                cp.start()


def _moe_up_kernel(te_ref, tv_ref, ci_ref, ne_ref, nc_ref, x_ref, wg_hbm, wu_hbm, o_ref,
                   wgf_sc, wuf_sc, wgb_sc, wub_sc, sem_g, sem_u, *, mi):
    i = pl.program_id(1)
    valid = tv_ref[i] != 0
    _expert_weights(te_ref, ci_ref, ne_ref, nc_ref, mi, o_ref.shape[1], (wg_hbm, wu_hbm),
                    (wgf_sc, wuf_sc), (wgb_sc, wub_sc), (sem_g, sem_u))

    @pl.when(valid)
    def _():
        x = x_ref[...]
        g = _dot(x, wgb_sc[...])
        u = _dot(x, wub_sc[...])
        o_ref[...] = (_silu(g) * u).astype(o_ref.dtype)

    @pl.when(jnp.logical_not(valid))
    def _():
        o_ref[...] = jnp.zeros(o_ref.shape, o_ref.dtype)


def _moe_down_kernel(te_ref, tv_ref, ci_ref, ne_ref, nc_ref, h_ref, wd_hbm, o_ref,
                     wdf_sc, wdb_sc, sem_d, *, mi):
    i = pl.program_id(1)
    valid = tv_ref[i] != 0
    _expert_weights(te_ref, ci_ref, ne_ref, nc_ref, mi, o_ref.shape[1], (wd_hbm,),
                    (wdf_sc,), (wdb_sc,), (sem_d,))

    @pl.when(valid)
    def _():
        o_ref[...] = _dot(h_ref[...], wdb_sc[...])

    @pl.when(jnp.logical_not(valid))
    def _():
        o_ref[...] = jnp.zeros(o_ref.shape, o_ref.dtype)


def _moe_combine_kernel(pos_ref, x_ref, g_ref, y_hbm, o_ref, rows_sc, sem):
    bt = x_ref.shape[0]
    _issue_row_gather(pos_ref, TOP_K * bt, y_hbm, rows_sc, sem)
    _wait_row_gather(TOP_K * bt, y_hbm, rows_sc, sem)
    g = g_ref[...]
    o_ref[...] = x_ref[...] + (g[:, 0:1] * rows_sc[0:bt, :] + g[:, 1:2] * rows_sc[bt:2 * bt, :])


def moe_ffn(xt, norm_w, router_w, w_gate, w_up, w_down, mi, bn=512, bn_up=256, bt=256):
    t, d = xt.shape
    _, n_experts, _, f = w_gate.shape
    bm = min(MOE_TILE, t)
    bt = min(bt, t)
    h, idx, gates = norm_router(xt, norm_w, router_w)

    e_flat = idx[:, :TOP_K].reshape(-1)
    onehot = (e_flat[:, None] == jnp.arange(n_experts, dtype=jnp.int32)[None, :]).astype(jnp.int32)
    csum = jnp.cumsum(onehot, axis=0)
    rank = jnp.sum((csum - onehot) * onehot, axis=1)
    counts = csum[-1]
    tiles_e = (counts + bm - 1) // bm
    tile_end = jnp.cumsum(tiles_e)
    row_start = (tile_end - tiles_e) * bm
    pos = row_start[e_flat] + rank
    n_tiles = (TOP_K * t) // bm + n_experts
    n_rows = n_tiles * bm
    row_token = jnp.zeros((n_rows,), jnp.int32).at[pos].set(jnp.arange(TOP_K * t, dtype=jnp.int32) // TOP_K)
    tile_ids = jnp.arange(n_tiles, dtype=jnp.int32)
    n_valid = tile_end[-1]
    tile_valid = (tile_ids < n_valid).astype(jnp.int32)
    last_valid = jnp.minimum(tile_ids, n_valid - 1)
    tile_expert = jnp.sum((last_valid[:, None] >= tile_end[None, :]).astype(jnp.int32), axis=1)
    tile_expert = jnp.minimum(tile_expert, n_experts - 1)

    tok_tiles = row_token.reshape(n_tiles, 1, bm)
    xs = pl.pallas_call(
        _moe_gather_kernel,
        grid_spec=pltpu.PrefetchScalarGridSpec(
            num_scalar_prefetch=1,
            grid=(n_tiles,),
            in_specs=[pl.BlockSpec((1, 1, bm), lambda i, tv: (i, 0, 0), memory_space=pltpu.SMEM),
                      pl.BlockSpec((1, 1, bm), lambda i, tv: (jnp.minimum(i + 1, n_tiles - 1), 0, 0),
                                   memory_space=pltpu.SMEM),
                      pl.BlockSpec(memory_space=pl.ANY)],
            out_specs=pl.BlockSpec((bm, d), lambda i, tv: (i, 0)),
            scratch_shapes=[pltpu.VMEM((2, bm, d), F32), pltpu.SemaphoreType.DMA((2,))]),
        out_shape=jax.ShapeDtypeStruct((n_rows, d), BF16),
        compiler_params=_params("arbitrary"),
        name="moe_gather",
    )(tile_valid, tok_tiles, tok_tiles, h)

    owns = tiles_e > 0
    eids = jnp.arange(n_experts, dtype=jnp.int32)
    later = jnp.logical_and(owns[None, :], eids[None, :] > eids[:, None])
    next_owner = jnp.min(jnp.where(later, eids[None, :], n_experts), axis=1)
    next_owner = jnp.where(next_owner == n_experts, -1, next_owner).astype(jnp.int32)
    owner_idx = (jnp.cumsum(owns.astype(jnp.int32)) - owns.astype(jnp.int32)).astype(jnp.int32)
    tile_ci = owner_idx[tile_expert]
    tile_ne = next_owner[tile_expert]
    n_owners = jnp.sum(owns.astype(jnp.int32)).reshape(1)
    route = (tile_expert, tile_valid, tile_ci, tile_ne, n_owners)
    dma2 = pltpu.SemaphoreType.DMA((2,))

    hid = pl.pallas_call(
        functools.partial(_moe_up_kernel, mi=mi),
        grid_spec=pltpu.PrefetchScalarGridSpec(
            num_scalar_prefetch=len(route),
            grid=(f // bn_up, n_tiles),
            in_specs=[pl.BlockSpec((bm, d), lambda j, i, *_: (i, 0)),
                      pl.BlockSpec(memory_space=pl.ANY),
                      pl.BlockSpec(memory_space=pl.ANY)],
            out_specs=pl.BlockSpec((bm, bn_up), lambda j, i, *_: (i, j)),
            scratch_shapes=[pltpu.VMEM((2, d, bn_up), F32), pltpu.VMEM((2, d, bn_up), F32),
                            pltpu.VMEM((d, bn_up), BF16), pltpu.VMEM((d, bn_up), BF16), dma2, dma2]),
        out_shape=jax.ShapeDtypeStruct((n_rows, f), BF16),
        compiler_params=_params("arbitrary", "arbitrary"),
        name="moe_up",
    )(*route, xs, w_gate, w_up)

    y = pl.pallas_call(
        functools.partial(_moe_down_kernel, mi=mi),
        grid_spec=pltpu.PrefetchScalarGridSpec(
            num_scalar_prefetch=len(route),
            grid=(d // bn, n_tiles),
            in_specs=[pl.BlockSpec((bm, f), lambda j, i, *_: (i, 0)),
                      pl.BlockSpec(memory_space=pl.ANY)],
            out_specs=pl.BlockSpec((bm, bn), lambda j, i, *_: (i, j)),
            scratch_shapes=[pltpu.VMEM((2, f, bn), F32), pltpu.VMEM((f, bn), BF16), dma2]),
        out_shape=jax.ShapeDtypeStruct((n_rows, d), F32),
        compiler_params=_params("arbitrary", "arbitrary"),
        name="moe_down",
    )(*route, hid, w_down)

    pos_tiles = pos.reshape(t // bt, bt, TOP_K).transpose(0, 2, 1).reshape(t // bt, 1, TOP_K * bt)
    return pl.pallas_call(
        _moe_combine_kernel,
        grid=(t // bt,),
        in_specs=[pl.BlockSpec((1, 1, TOP_K * bt), lambda i: (i, 0, 0), memory_space=pltpu.SMEM),
                  pl.BlockSpec((bt, d), lambda i: (i, 0)),
                  pl.BlockSpec((bt, LANES), lambda i: (i, 0)),
                  pl.BlockSpec(memory_space=pl.ANY)],
        out_specs=pl.BlockSpec((bt, d), lambda i: (i, 0)),
        out_shape=jax.ShapeDtypeStruct((t, d), F32),
        scratch_shapes=[pltpu.VMEM((TOP_K * bt, d), F32), pltpu.SemaphoreType.DMA(())],
        compiler_params=_params("arbitrary"),
        name="moe_combine",
    )(pos_tiles, xt, gates, y)


def _lambda_init(layer):
    return 0.8 - 0.6 * math.exp(-0.3 * layer)


def kernel(x, norm1_w, w_in, w_out, q_norm_w, k_norm_w, lambda_q1, lambda_k1, lambda_q2, lambda_k2,
           subln_w, conv_w, conv_b, dt_bias, a_log, d_skip, ssm_norm_w, norm2_w, ffn_w_gate, ffn_w_up,
           ffn_w_down, router_w, moe_w_gate, moe_w_up, moe_w_down):
    b, s, d = x.shape
    t = b * s
    depth = w_in.shape[0]
    ssm_width = ssm_norm_w.shape[1]
    conv_ch = conv_w.shape[2]
    ssm_heads = dt_bias.shape[1]
    att_width = (w_in.shape[2] - ssm_width - conv_ch - ssm_heads) // 3
    n_qk_heads = att_width // ATT_HEAD_DIM
    q_scale = ATT_HEAD_DIM ** -0.5 * math.log2(math.e)
    bn = 512
    z0 = 3 * att_width
    c0 = z0 + ssm_width
    nz, nx = ssm_width // bn, conv_ch // bn

    xt = x.reshape(t, d)
    w_in_t = jnp.swapaxes(w_in, 1, 2)
    for layer in range(depth):

        h = rmsnorm(xt, norm1_w[layer])
        gain = jnp.concatenate([jnp.tile(q_norm_w[layer] * q_scale, n_qk_heads),
                                jnp.tile(k_norm_w[layer], n_qk_heads)]).reshape(1, 2 * att_width)
        qk = matmul_ws(h, w_in_t, layer, 0, 2 * att_width, BF16, gain=gain, bn=bn, name="proj_qk")
        v = matmul_ws(h, w_in_t, layer, 2 * att_width, att_width, BF16, bn=bn, name="proj_v")
        cz = matmul_ws(h, w_in_t, layer, z0, ssm_width + conv_ch, F32, bn=bn, name="proj_ssm",
                       out_block=lambda j: jnp.where(j < nz, j + nx, j - nz))
        dt = matmul_narrow(h, w_in_t, layer, c0 + conv_ch, ssm_heads, name="proj_dt")

        lam_vecs = jnp.stack([lambda_q1[layer], lambda_k1[layer], lambda_q2[layer], lambda_k2[layer]])
        att = diff_attention(qk, v, lam_vecs, subln_w[layer], _lambda_init(layer), b, s)
        ssm = ssd_mixer(cz, dt, conv_w[layer], conv_b[layer], dt_bias[layer], a_log[layer],
                        d_skip[layer], ssm_norm_w[layer], b, s)
        xt = matmul_ws_res2(att, ssm, w_out, layer, xt, bn=bn)

        i = layer // 2
        if layer % 2 == 0:
            f = ffn_w_gate.shape[2]
            h2 = rmsnorm(xt, norm2_w[layer])
            hid = swiglu_up_ws(h2, ffn_w_gate, ffn_w_up, i)
            xt = matmul_residual_ktiled(hid, ffn_w_down[i].astype(BF16), xt, bk=f // 2, name="ffn_down")
        else:
            xt = moe_ffn(xt, norm2_w[layer], router_w[i], moe_w_gate, moe_w_up, moe_w_down, i)
    return xt.reshape(b, s, d)
```

```python
import functools
import math

import jax
import jax.numpy as jnp
from jax import lax
from jax.experimental import pallas as pl
from jax.experimental.pallas import tpu as pltpu

F32 = jnp.float32
BF16 = jnp.bfloat16
EPS = 1e-6
LANES = 128
VMEM_LIMIT = 56 * 1024 * 1024
VMEM_LIMIT_BIG = 60 * 1024 * 1024

ATT_HEAD_DIM = 128
SSM_HEAD_DIM = 64
SSM_GROUPS = 8
SSM_STATE = 128
CONV_K = 4
SSM_CHUNK = 128
TOP_K = 2
MOE_TILE = 512


def _params(*sem, vmem=VMEM_LIMIT):
    return pltpu.CompilerParams(dimension_semantics=sem, vmem_limit_bytes=vmem)


def _dot(a, b):
    return jnp.dot(a, b, preferred_element_type=F32)


def _dot_nt(a, b):
    return lax.dot_general(a, b, (((1,), (1,)), ((), ())), preferred_element_type=F32)


def _silu(x):
    return x / (1.0 + jnp.exp(-x))


def _split3(x):
    hi = x.astype(BF16)
    r1 = x - hi.astype(F32)
    mid = r1.astype(BF16)
    lo = (r1 - mid.astype(F32)).astype(BF16)
    return hi, mid, lo


def _rmsnorm_kernel(x_ref, w_ref, o_ref):
    x = x_ref[...]
    ms = jnp.mean(x * x, axis=-1, keepdims=True)
    o_ref[...] = (x * lax.rsqrt(ms + EPS) * w_ref[...]).astype(o_ref.dtype)


def rmsnorm(x, w, out_dtype=BF16, bt=256):
    t, d = x.shape
    bt = min(bt, t)
    return pl.pallas_call(
        _rmsnorm_kernel,
        grid=(t // bt,),
        in_specs=[pl.BlockSpec((bt, d), lambda i: (i, 0)),
                  pl.BlockSpec((1, d), lambda i: (0, 0))],
        out_specs=pl.BlockSpec((bt, d), lambda i: (i, 0)),
        out_shape=jax.ShapeDtypeStruct((t, d), out_dtype),
        compiler_params=_params("parallel"),
        name="rmsnorm",
    )(x, w.reshape(1, d))


def _cast_weight(w_ref, wb_sc):
    @pl.when(pl.program_id(1) == 0)
    def _():
        wb_sc[...] = w_ref[...].astype(BF16)


def _ws_kernel(x_ref, w_ref, o_ref, wb_sc, *, dot):
    _cast_weight(w_ref, wb_sc)
    o_ref[...] = dot(x_ref[...], wb_sc[...]).astype(o_ref.dtype)


def _ws_headnorm_kernel(x_ref, w_ref, g_ref, o_ref, wb_sc, *, dot):
    _cast_weight(w_ref, wb_sc)
    acc = dot(x_ref[...], wb_sc[...])
    for c in range(acc.shape[1] // ATT_HEAD_DIM):
        sl = slice(c * ATT_HEAD_DIM, (c + 1) * ATT_HEAD_DIM)
        a = acc[:, sl]
        ms = jnp.mean(a * a, axis=-1, keepdims=True)
        o_ref[:, sl] = (a * lax.rsqrt(ms + EPS) * g_ref[:, sl]).astype(o_ref.dtype)


def _ws_res2_kernel(x1_ref, x2_ref, w_ref, r_ref, o_ref, wb_sc):
    _cast_weight(w_ref, wb_sc)
    k1 = x1_ref.shape[1]
    o_ref[...] = (r_ref[...] + _dot(x1_ref[...], wb_sc[0:k1, :])) + _dot(x2_ref[...], wb_sc[k1:, :])


def _ws_swiglu_kernel(x_ref, wg_ref, wu_ref, o_ref, wgb_sc, wub_sc):
    _cast_weight(wg_ref, wgb_sc)
    _cast_weight(wu_ref, wub_sc)
    x = x_ref[...]
    g = _dot(x, wgb_sc[...])
    u = _dot(x, wub_sc[...])
    o_ref[...] = (_silu(g) * u).astype(o_ref.dtype)


def matmul_ws(x, wt, layer, col0, ncols, out_dtype, gain=None, out_block=None, bm=1024, bn=512, name="proj"):
    m, k = x.shape
    bm = min(bm, m)
    assert m % bm == 0 and ncols % bn == 0 and col0 % bn == 0
    j0 = col0 // bn
    ob = out_block if out_block is not None else (lambda j: j)
    in_specs = [pl.BlockSpec((bm, k), lambda j, i: (i, 0)),
                pl.BlockSpec((None, bn, k), lambda j, i: (layer, j0 + j, 0))]
    args = [x, wt]
    kern = _ws_kernel
    if gain is not None:
        in_specs.append(pl.BlockSpec((1, bn), lambda j, i: (0, j)))
        args.append(gain)
        kern = _ws_headnorm_kernel
    return pl.pallas_call(
        functools.partial(kern, dot=_dot_nt),
        grid=(ncols // bn, m // bm),
        in_specs=in_specs,
        out_specs=pl.BlockSpec((bm, bn), lambda j, i: (i, ob(j))),
        out_shape=jax.ShapeDtypeStruct((m, ncols), out_dtype),
        scratch_shapes=[pltpu.VMEM((bn, k), BF16)],
        compiler_params=_params("arbitrary", "arbitrary"),
        name=name,
    )(*args)


def matmul_ws_res2(x1, x2, w, layer, res, bm=1024, bn=512, name="out_proj"):
    m, k1 = x1.shape
    k2 = x2.shape[1]
    n = w.shape[2]
    bm = min(bm, m)
    assert m % bm == 0 and n % bn == 0 and w.shape[1] == k1 + k2
    return pl.pallas_call(
        _ws_res2_kernel,
        grid=(n // bn, m // bm),
        in_specs=[pl.BlockSpec((bm, k1), lambda j, i: (i, 0)),
                  pl.BlockSpec((bm, k2), lambda j, i: (i, 0)),
                  pl.BlockSpec((None, k1 + k2, bn), lambda j, i: (layer, 0, j)),
                  pl.BlockSpec((bm, bn), lambda j, i: (i, j))],
        out_specs=pl.BlockSpec((bm, bn), lambda j, i: (i, j)),
        out_shape=jax.ShapeDtypeStruct((m, n), F32),
        scratch_shapes=[pltpu.VMEM((k1 + k2, bn), BF16)],
        compiler_params=_params("arbitrary", "arbitrary"),
        name=name,
    )(x1, x2, w, res)


def swiglu_up_ws(x, wg, wu, layer, bm=1024, bn=256):
    m, k = x.shape
    n = wg.shape[2]
    bm = min(bm, m)
    assert m % bm == 0 and n % bn == 0
    return pl.pallas_call(
        _ws_swiglu_kernel,
        grid=(n // bn, m // bm),
        in_specs=[pl.BlockSpec((bm, k), lambda j, i: (i, 0)),
                  pl.BlockSpec((None, k, bn), lambda j, i: (layer, 0, j)),
                  pl.BlockSpec((None, k, bn), lambda j, i: (layer, 0, j))],
        out_specs=pl.BlockSpec((bm, bn), lambda j, i: (i, j)),
        out_shape=jax.ShapeDtypeStruct((m, n), BF16),
        scratch_shapes=[pltpu.VMEM((k, bn), BF16), pltpu.VMEM((k, bn), BF16)],
        compiler_params=_params("arbitrary", "arbitrary"),
        name="swiglu_up",
    )(x, wg, wu)


def _mm_res_ktiled_kernel(x_ref, w_ref, r_ref, o_ref):
    k = pl.program_id(2)

    @pl.when(k == 0)
    def _():
        o_ref[...] = r_ref[...] + _dot(x_ref[...], w_ref[...])

    @pl.when(k != 0)
    def _():
        o_ref[...] += _dot(x_ref[...], w_ref[...])


def _narrow_kernel(x_ref, w_ref, o_ref, wb_sc):
    @pl.when(pl.program_id(0) == 0)
    def _():
        wb_sc[...] = jnp.zeros(wb_sc.shape, BF16)
        wb_sc[0:w_ref.shape[0], :] = w_ref[...].astype(BF16)

    o_ref[...] = _dot_nt(x_ref[...], wb_sc[...])


def matmul_narrow(x, wt, layer, row0, nrows, bm=1024, name="proj_narrow"):
    m, k = x.shape
    bm = min(bm, m)
    assert m % bm == 0 and row0 % nrows == 0 and nrows % 16 == 0 and nrows <= LANES
    return pl.pallas_call(
        _narrow_kernel,
        grid=(m // bm,),
        in_specs=[pl.BlockSpec((bm, k), lambda i: (i, 0)),
                  pl.BlockSpec((None, nrows, k), lambda i: (layer, row0 // nrows, 0))],
        out_specs=pl.BlockSpec((bm, LANES), lambda i: (i, 0)),
        out_shape=jax.ShapeDtypeStruct((m, LANES), F32),
        scratch_shapes=[pltpu.VMEM((LANES, k), BF16)],
        compiler_params=_params("arbitrary"),
        name=name,
    )(x, wt)


def matmul_residual_ktiled(x, w, res, bk, bm=1024, bn=512, name="matmul_residual"):
    m, k = x.shape
    n = w.shape[1]
    bm, bn = min(bm, m), min(bn, n)
    assert m % bm == 0 and n % bn == 0 and k % bk == 0
    return pl.pallas_call(
        _mm_res_ktiled_kernel,
        grid=(m // bm, n // bn, k // bk),
        in_specs=[pl.BlockSpec((bm, bk), lambda i, j, kk: (i, kk)),
                  pl.BlockSpec((bk, bn), lambda i, j, kk: (kk, j)),
                  pl.BlockSpec((bm, bn), lambda i, j, kk: (i, j))],
        out_specs=pl.BlockSpec((bm, bn), lambda i, j, kk: (i, j)),
        out_shape=jax.ShapeDtypeStruct((m, n), F32),
        compiler_params=_params("parallel", "arbitrary", "arbitrary"),
        name=name,
    )(x, w, res)


def _attn_kernel(lam_ref, q_ref, k_ref, v_ref, w_ref, o_ref, m_sc, l_sc, acc_sc, *, blk, nsub, lam_init):
    d = ATT_HEAD_DIM
    nt = blk // LANES
    qi = pl.program_id(2)

    m_sc[...] = jnp.full(m_sc.shape, -jnp.inf, F32)
    l_sc[...] = jnp.zeros(l_sc.shape, F32)
    acc_sc[...] = jnp.zeros(acc_sc.shape, F32)

    row = lax.broadcasted_iota(jnp.int32, (blk, blk), 0)
    col = lax.broadcasted_iota(jnp.int32, (blk, blk), 1)
    keep = col <= row

    def chain(sub, c, k, v, masked):
        ch = sub * 2 + c
        q = q_ref[sub * blk:(sub + 1) * blk, c * d:(c + 1) * d]
        s = _dot_nt(q, k[:, c * d:(c + 1) * d])
        if masked:
            s = jnp.where(keep, s, -jnp.inf)
        tiles = [s[:, j * LANES:(j + 1) * LANES] for j in range(nt)]
        tile_max = functools.reduce(jnp.maximum, tiles)
        m_prev = m_sc[ch]
        m_new = jnp.maximum(m_prev, jnp.max(tile_max, axis=-1, keepdims=True))
        alpha = jnp.exp2(m_prev - m_new)
        ps = [jnp.exp2(tl - m_new) for tl in tiles]
        l_sc[ch] = alpha * l_sc[ch] + functools.reduce(jnp.add, ps)
        m_sc[ch] = m_new
        pv = _dot(jnp.concatenate(ps, axis=1).astype(BF16), v)
        for j in range(pv.shape[1] // LANES):
            sl = slice(j * LANES, (j + 1) * LANES)
            acc_sc[ch, :, sl] = alpha * acc_sc[ch, :, sl] + pv[:, sl]

    def step(kb, first_sub, diag_sub):
        start = pl.multiple_of(kb * blk, blk)
        k = k_ref[pl.ds(start, blk), :]
        v = v_ref[pl.ds(start, blk), :]
        for sub in range(first_sub, nsub):
            for c in range(2):
                chain(sub, c, k, v, sub == diag_sub)

    def body(kb, carry):
        step(kb, 0, None)
        return carry

    lax.fori_loop(0, qi * nsub, body, 0)
    for sub in range(nsub):
        step(qi * nsub + sub, sub, sub)

    lv = lam_ref[...]
    a1 = jnp.sum(lv[0:1] * lv[1:2], axis=-1, keepdims=True)
    a2 = jnp.sum(lv[2:3] * lv[3:4], axis=-1, keepdims=True)
    lam = jnp.exp(a1) - jnp.exp(a2) + lam_init
    gain = w_ref[...] * (1.0 - lam_init)
    for sub in range(nsub):
        l1 = jnp.sum(l_sc[2 * sub], axis=-1, keepdims=True)
        l2 = jnp.sum(l_sc[2 * sub + 1], axis=-1, keepdims=True)
        o = acc_sc[2 * sub] / l1 - lam * (acc_sc[2 * sub + 1] / l2)
        ms = jnp.mean(o * o, axis=-1, keepdims=True)
        o_ref[sub * blk:(sub + 1) * blk, :] = (o * lax.rsqrt(ms + EPS) * gain).astype(o_ref.dtype)


def diff_attention(qk, v, lam_vecs, subln_w, lam_init, batch, seq, blk=512, nsub=4):
    t, width = v.shape
    hv = 2 * ATT_HEAD_DIM
    heads = width // hv
    blk = min(blk, seq)
    nsub = min(nsub, seq // blk)
    bq = blk * nsub
    nq = seq // bq
    kern = functools.partial(_attn_kernel, blk=blk, nsub=nsub, lam_init=lam_init)
    return pl.pallas_call(
        kern,
        grid=(batch, heads, nq),
        in_specs=[pl.BlockSpec((4, ATT_HEAD_DIM), lambda b, h, i: (0, 0)),
                  pl.BlockSpec((bq, hv), lambda b, h, i: (b * nq + i, h)),
                  pl.BlockSpec((seq, hv), lambda b, h, i: (b, heads + h)),
                  pl.BlockSpec((seq, hv), lambda b, h, i: (b, h)),
                  pl.BlockSpec((1, hv), lambda b, h, i: (0, 0))],
        out_specs=pl.BlockSpec((bq, hv), lambda b, h, i: (b * nq + i, h)),
        out_shape=jax.ShapeDtypeStruct((t, width), BF16),
        scratch_shapes=[pltpu.VMEM((2 * nsub, blk, LANES), F32),
                        pltpu.VMEM((2 * nsub, blk, LANES), F32),
                        pltpu.VMEM((2 * nsub, blk, hv), F32)],
        compiler_params=_params("parallel", "parallel", "arbitrary"),
        name="diff_attention",
    )(lam_vecs, qk, qk, v, subln_w.reshape(1, hv))


def _ssd_kernel(xbc_ref, z_ref, dt_ref, cw_ref, cb_ref, dtb_ref, alog_ref, dsk_ref, nw_ref,
                o_ref, ext_sc, state_sc, *, width):
    L = SSM_CHUNK
    n = SSM_STATE
    gw = width // SSM_GROUPS
    hpg = gw // SSM_HEAD_DIM
    pad = 8
    c = pl.program_id(1)

    @pl.when(c == 0)
    def _():
        ext_sc[0:pad, :] = jnp.zeros((pad, ext_sc.shape[1]), F32)
        state_sc[...] = jnp.zeros(state_sc.shape, F32)

    ext_sc[pad:pad + L, :] = xbc_ref[...]

    def conv_silu(lo, hi):
        a = ext_sc[:, lo:hi]
        acc = cb_ref[:, lo:hi] + a[pad:pad + L] * cw_ref[CONV_K - 1:CONV_K, lo:hi]
        for tap in range(CONV_K - 1):
            shifted = pltpu.roll(a, CONV_K - 1 - tap, axis=0)[pad:pad + L]
            acc = acc + shifted * cw_ref[tap:tap + 1, lo:hi]
        return _silu(acc)

    dtr = dt_ref[...] + dtb_ref[...]
    dtp = jnp.maximum(dtr, 0.0) + jnp.log1p(jnp.exp(-jnp.abs(dtr)))
    a = -jnp.exp(alog_ref[...])
    d_a = dtp * a
    row = lax.broadcasted_iota(jnp.int32, (L, L), 0)
    col = lax.broadcasted_iota(jnp.int32, (L, L), 1)
    causal = col <= row
    tril = jnp.where(causal, 1.0, 0.0).astype(BF16)
    hi_, mid_, lo_ = _split3(d_a)
    acum = _dot(tril, hi_) + _dot(tril, mid_) + _dot(tril, lo_)
    acum_t = acum.T
    dtp_t = dtp.T
    e_acum = jnp.exp(acum)
    a_last = acum[L - 1:L, :]
    to_end = jnp.exp(a_last - acum) * dtp
    e_last = jnp.exp(a_last)

    head_of_col = lax.broadcasted_iota(jnp.int32, (1, gw), 1) // SSM_HEAD_DIM

    def expand(x, g):
        out = x[:, g * hpg:g * hpg + 1]
        for hd in range(1, hpg):
            out = jnp.where(head_of_col == hd, x[:, g * hpg + hd:g * hpg + hd + 1], out)
        return jnp.broadcast_to(out, (x.shape[0], gw))

    for g in range(SSM_GROUPS):
        xg = conv_silu(g * gw, (g + 1) * gw)
        bg = conv_silu(width + g * n, width + (g + 1) * n)
        cg = conv_silu(width + SSM_GROUPS * n + g * n, width + SSM_GROUPS * n + (g + 1) * n)
        xg16 = xg.astype(BF16)
        cg16 = cg.astype(BF16)
        cb = _dot_nt(cg16, bg.astype(BF16))
        y = None
        for hd in range(hpg):
            h = g * hpg + hd
            seg = acum[:, h:h + 1] - acum_t[h:h + 1, :]
            decay = jnp.exp(jnp.where(causal, seg, -jnp.inf))
            w = cb * decay * dtp_t[h:h + 1, :]
            yh = _dot(w.astype(BF16), xg16)
            y = yh if y is None else jnp.where(head_of_col == hd, yh, y)
        st = state_sc[g]
        y = y + _dot(cg16, st.astype(BF16)) * expand(e_acum, g)
        xw = (xg * expand(to_end, g)).astype(BF16)
        state_sc[g] = st * expand(e_last, g) + _dot(bg.T.astype(BF16), xw)
        y = y + dsk_ref[:, g * gw:(g + 1) * gw] * xg
        gated = y * _silu(z_ref[:, g * gw:(g + 1) * gw])
        ms = jnp.mean(gated * gated, axis=-1, keepdims=True)
        o_ref[:, g * gw:(g + 1) * gw] = (gated * lax.rsqrt(ms + EPS)
                                         * nw_ref[:, g * gw:(g + 1) * gw]).astype(o_ref.dtype)

    ext_sc[0:pad, :] = xbc_ref[L - pad:L, :]


def ssd_mixer(cz, dt, conv_w, conv_b, dt_bias, a_log, d_skip, norm_w, batch, seq):
    t = cz.shape[0]
    conv_ch = conv_w.shape[1]
    width = norm_w.shape[0]
    heads = width // SSM_HEAD_DIM
    L = SSM_CHUNK
    nc = seq // L
    assert cz.shape[1] == conv_ch + width and conv_ch % width == 0
    z_blk = conv_ch // width
    pad_heads = lambda v: jnp.pad(v.reshape(1, heads), ((0, 0), (0, LANES - heads)))
    dsk = jnp.repeat(d_skip, SSM_HEAD_DIM).reshape(1, width)
    return pl.pallas_call(
        functools.partial(_ssd_kernel, width=width),
        grid=(batch, nc),
        in_specs=[pl.BlockSpec((L, conv_ch), lambda b, c: (b * nc + c, 0)),
                  pl.BlockSpec((L, width), lambda b, c: (b * nc + c, z_blk)),
                  pl.BlockSpec((L, LANES), lambda b, c: (b * nc + c, 0)),
                  pl.BlockSpec((CONV_K, conv_ch), lambda b, c: (0, 0)),
                  pl.BlockSpec((1, conv_ch), lambda b, c: (0, 0)),
                  pl.BlockSpec((1, LANES), lambda b, c: (0, 0)),
                  pl.BlockSpec((1, LANES), lambda b, c: (0, 0)),
                  pl.BlockSpec((1, width), lambda b, c: (0, 0)),
                  pl.BlockSpec((1, width), lambda b, c: (0, 0))],
        out_specs=pl.BlockSpec((L, width), lambda b, c: (b * nc + c, 0)),
        out_shape=jax.ShapeDtypeStruct((t, width), BF16),
        scratch_shapes=[pltpu.VMEM((8 + L, conv_ch), F32),
                        pltpu.VMEM((SSM_GROUPS, SSM_STATE, width // SSM_GROUPS), F32)],
        compiler_params=_params("parallel", "arbitrary"),
        name="ssd_mixer",
    )(cz, cz, dt, conv_w, conv_b.reshape(1, conv_ch), pad_heads(dt_bias), pad_heads(a_log), dsk,
      norm_w.reshape(1, width))


def _norm_router_kernel(x_ref, w_ref, rw_ref, h_ref, idx_ref, gate_ref, *, n_experts):
    x = x_ref[...]
    ms = jnp.mean(x * x, axis=-1, keepdims=True)
    hn = x * lax.rsqrt(ms + EPS) * w_ref[...]
    half = hn.shape[1] // 2
    hb = hn.astype(BF16).astype(F32)
    hi = pltpu.bitcast(hb[:, :half], jnp.uint32)
    lo = pltpu.bitcast(hb[:, half:], jnp.uint32)
    h_ref[...] = hi | (lo >> 16)
    hh, hm, _ = _split3(hn)
    rh, rm, _ = _split3(rw_ref[...])
    logits = _dot(hh, rh) + _dot(hh, rm) + _dot(hm, rh)
    lane = lax.broadcasted_iota(jnp.int32, logits.shape, 1)
    lanef = lane.astype(F32)
    logits = jnp.where(lane < n_experts, logits, -jnp.inf)
    m1 = jnp.max(logits, axis=-1, keepdims=True)
    i1 = jnp.min(jnp.where(logits == m1, lanef, float(LANES)), axis=-1, keepdims=True)
    rest = jnp.where(lanef == i1, -jnp.inf, logits)
    m2 = jnp.max(rest, axis=-1, keepdims=True)
    i2 = jnp.min(jnp.where(rest == m2, lanef, float(LANES)), axis=-1, keepdims=True)
    e = jnp.exp(m2 - m1)
    g1 = 1.0 / (1.0 + e)
    idx_ref[...] = jnp.where(lane == 0, i1, i2).astype(jnp.int32)
    gate_ref[...] = jnp.where(lane == 0, g1, e * g1)


def norm_router(x, w, router_w, bt=256):
    t, d = x.shape
    n_experts = router_w.shape[1]
    bt = min(bt, t)
    rw = jnp.pad(router_w, ((0, 0), (0, LANES - n_experts)))
    return pl.pallas_call(
        functools.partial(_norm_router_kernel, n_experts=n_experts),
        grid=(t // bt,),
        in_specs=[pl.BlockSpec((bt, d), lambda i: (i, 0)),
                  pl.BlockSpec((1, d), lambda i: (0, 0)),
                  pl.BlockSpec((d, LANES), lambda i: (0, 0))],
        out_specs=[pl.BlockSpec((bt, d // 2), lambda i: (i, 0)),
                   pl.BlockSpec((bt, LANES), lambda i: (i, 0)),
                   pl.BlockSpec((bt, LANES), lambda i: (i, 0))],
        out_shape=[jax.ShapeDtypeStruct((t, d // 2), jnp.uint32),
                   jax.ShapeDtypeStruct((t, LANES), jnp.int32),
                   jax.ShapeDtypeStruct((t, LANES), F32)],
        compiler_params=_params("parallel"),
        name="norm_router",
    )(x, w.reshape(1, d), rw)


def _issue_row_gather(idx_ref, n_rows, src_hbm, dst_ref, sem):
    def issue(r, carry):
        pltpu.make_async_copy(src_hbm.at[pl.ds(idx_ref[0, 0, r], 1), :],
                              dst_ref.at[pl.ds(r, 1), :], sem).start()
        return carry

    lax.fori_loop(0, n_rows, issue, 0)


def _wait_row_gather(n_rows, src_hbm, dst_ref, sem):
    pltpu.make_async_copy(src_hbm.at[pl.ds(0, n_rows), :], dst_ref, sem).wait()


def _moe_gather_kernel(tv_ref, tok_ref, tok_next_ref, h_hbm, o_ref, rows_sc, sems):
    i = pl.program_id(0)
    n = pl.num_programs(0)
    bm = o_ref.shape[0]
    slot = lax.rem(i, 2)

    @pl.when(jnp.logical_and(i == 0, tv_ref[0] != 0))
    def _():
        _issue_row_gather(tok_ref, bm, h_hbm, rows_sc.at[0], sems.at[0])

    nxt = jnp.minimum(i + 1, n - 1)

    @pl.when(jnp.logical_and(i + 1 < n, tv_ref[nxt] != 0))
    def _():
        _issue_row_gather(tok_next_ref, bm, h_hbm, rows_sc.at[1 - slot], sems.at[1 - slot])

    @pl.when(tv_ref[i] != 0)
    def _():
        _wait_row_gather(bm, h_hbm, rows_sc.at[slot], sems.at[slot])
        words = rows_sc[slot]
        half = words.shape[1]
        o_ref[:, :half] = pltpu.bitcast(words & jnp.uint32(0xFFFF0000), F32).astype(o_ref.dtype)
        o_ref[:, half:] = pltpu.bitcast(words << 16, F32).astype(o_ref.dtype)

    @pl.when(tv_ref[i] == 0)
    def _():
        o_ref[...] = jnp.zeros(o_ref.shape, o_ref.dtype)


def _expert_weights(te_ref, ci_ref, ne_ref, nc_ref, mi, bn, w_hbms, wf_scs, wb_scs, sems):
    j = pl.program_id(0)
    i = pl.program_id(1)
    nj = pl.num_programs(0)
    changed = jnp.logical_or(i == 0, te_ref[i] != te_ref[jnp.maximum(i - 1, 0)])

    def copies(e, jj):
        col = pl.multiple_of(jj * bn, bn)
        return [pltpu.make_async_copy(w.at[mi, e, :, pl.ds(col, bn)], wf, sem)
                for w, wf, sem in zip(w_hbms, wf_scs, sems)]

    @pl.when(changed)
    def _():
        @pl.when(j * nc_ref[0] + ci_ref[i] == 0)
        def _():
            for cp in copies(te_ref[0], 0):
                cp.start()

        for cp in copies(te_ref[i], j):
            cp.wait()
        for wf, wb in zip(wf_scs, wb_scs):
            wb[...] = wf[...].astype(BF16)

        more_here = ne_ref[i] >= 0

        @pl.when(more_here)
        def _():
            for cp in copies(ne_ref[i], j):
                cp.start()

        @pl.when(jnp.logical_and(jnp.logical_not(more_here), j + 1 < nj))
        def _():
            for cp in copies(te_ref[0], j + 1):
                cp.start()


def _moe_up_kernel(te_ref, tv_ref, ci_ref, ne_ref, nc_ref, x_ref, wg_hbm, wu_hbm, o_ref,
                   wgf_sc, wuf_sc, wgb_sc, wub_sc, sem_g, sem_u, *, mi):
    i = pl.program_id(1)
    valid = tv_ref[i] != 0
    _expert_weights(te_ref, ci_ref, ne_ref, nc_ref, mi, o_ref.shape[1], (wg_hbm, wu_hbm),
                    (wgf_sc, wuf_sc), (wgb_sc, wub_sc), (sem_g, sem_u))

    @pl.when(valid)
    def _():
        x = x_ref[...]
        g = _dot(x, wgb_sc[...])
        u = _dot(x, wub_sc[...])
        o_ref[...] = (_silu(g) * u).astype(o_ref.dtype)

    @pl.when(jnp.logical_not(valid))
    def _():
        o_ref[...] = jnp.zeros(o_ref.shape, o_ref.dtype)


def _moe_down_kernel(te_ref, tv_ref, ci_ref, ne_ref, nc_ref, h_ref, wd_hbm, o_ref,
                     wdf_sc, wdb_sc, sem_d, *, mi):
    i = pl.program_id(1)
    valid = tv_ref[i] != 0
    _expert_weights(te_ref, ci_ref, ne_ref, nc_ref, mi, o_ref.shape[1], (wd_hbm,),
                    (wdf_sc,), (wdb_sc,), (sem_d,))

    @pl.when(valid)
    def _():
        o_ref[...] = _dot(h_ref[...], wdb_sc[...])

    @pl.when(jnp.logical_not(valid))
    def _():
        o_ref[...] = jnp.zeros(o_ref.shape, o_ref.dtype)


def _moe_combine_kernel(pos_ref, x_ref, g_ref, y_hbm, o_ref, rows_sc, sem):
    bt = x_ref.shape[0]
    _issue_row_gather(pos_ref, TOP_K * bt, y_hbm, rows_sc, sem)
    _wait_row_gather(TOP_K * bt, y_hbm, rows_sc, sem)
    g = g_ref[...]
    o_ref[...] = x_ref[...] + (g[:, 0:1] * rows_sc[0:bt, :] + g[:, 1:2] * rows_sc[bt:2 * bt, :])


def moe_ffn(xt, norm_w, router_w, w_gate, w_up, w_down, mi, bn=512, bn_up=512, bt=256):
    t, d = xt.shape
    _, n_experts, _, f = w_gate.shape
    bm = min(MOE_TILE, t)
    bt = min(bt, t)
    h, idx, gates = norm_router(xt, norm_w, router_w)

    e_flat = idx[:, :TOP_K].reshape(-1)
    onehot = (e_flat[:, None] == jnp.arange(n_experts, dtype=jnp.int32)[None, :]).astype(jnp.int32)
    csum = jnp.cumsum(onehot, axis=0)
    rank = jnp.sum((csum - onehot) * onehot, axis=1)
    counts = csum[-1]
    tiles_e = (counts + bm - 1) // bm
    tile_end = jnp.cumsum(tiles_e)
    row_start = (tile_end - tiles_e) * bm
    pos = row_start[e_flat] + rank
    n_tiles = (TOP_K * t) // bm + n_experts
    n_rows = n_tiles * bm
    row_token = jnp.zeros((n_rows,), jnp.int32).at[pos].set(jnp.arange(TOP_K * t, dtype=jnp.int32) // TOP_K)
    tile_ids = jnp.arange(n_tiles, dtype=jnp.int32)
    n_valid = tile_end[-1]
    tile_valid = (tile_ids < n_valid).astype(jnp.int32)
    last_valid = jnp.minimum(tile_ids, n_valid - 1)
    tile_expert = jnp.sum((last_valid[:, None] >= tile_end[None, :]).astype(jnp.int32), axis=1)
    tile_expert = jnp.minimum(tile_expert, n_experts - 1)

    tok_tiles = row_token.reshape(n_tiles, 1, bm)
    xs = pl.pallas_call(
        _moe_gather_kernel,
        grid_spec=pltpu.PrefetchScalarGridSpec(
            num_scalar_prefetch=1,
            grid=(n_tiles,),
            in_specs=[pl.BlockSpec((1, 1, bm), lambda i, tv: (i, 0, 0), memory_space=pltpu.SMEM),
                      pl.BlockSpec((1, 1, bm), lambda i, tv: (jnp.minimum(i + 1, n_tiles - 1), 0, 0),
                                   memory_space=pltpu.SMEM),
                      pl.BlockSpec(memory_space=pl.ANY)],
            out_specs=pl.BlockSpec((bm, d), lambda i, tv: (i, 0)),
            scratch_shapes=[pltpu.VMEM((2, bm, d // 2), jnp.uint32), pltpu.SemaphoreType.DMA((2,))]),
        out_shape=jax.ShapeDtypeStruct((n_rows, d), BF16),
        compiler_params=_params("arbitrary"),
        name="moe_gather",
    )(tile_valid, tok_tiles, tok_tiles, h)

    owns = tiles_e > 0
    eids = jnp.arange(n_experts, dtype=jnp.int32)
    later = jnp.logical_and(owns[None, :], eids[None, :] > eids[:, None])
    next_owner = jnp.min(jnp.where(later, eids[None, :], n_experts), axis=1)
    next_owner = jnp.where(next_owner == n_experts, -1, next_owner).astype(jnp.int32)
    owner_idx = (jnp.cumsum(owns.astype(jnp.int32)) - owns.astype(jnp.int32)).astype(jnp.int32)
    tile_ci = owner_idx[tile_expert]
    tile_ne = next_owner[tile_expert]
    n_owners = jnp.sum(owns.astype(jnp.int32)).reshape(1)
    route = (tile_expert, tile_valid, tile_ci, tile_ne, n_owners)
    dma = pltpu.SemaphoreType.DMA(())

    hid = pl.pallas_call(
        functools.partial(_moe_up_kernel, mi=mi),
        grid_spec=pltpu.PrefetchScalarGridSpec(
            num_scalar_prefetch=len(route),
            grid=(f // bn_up, n_tiles),
            in_specs=[pl.BlockSpec((bm, d), lambda j, i, *_: (i, 0)),
                      pl.BlockSpec(memory_space=pl.ANY),
                      pl.BlockSpec(memory_space=pl.ANY)],
            out_specs=pl.BlockSpec((bm, bn_up), lambda j, i, *_: (i, j)),
            scratch_shapes=[pltpu.VMEM((d, bn_up), F32), pltpu.VMEM((d, bn_up), F32),
                            pltpu.VMEM((d, bn_up), BF16), pltpu.VMEM((d, bn_up), BF16), dma, dma]),
        out_shape=jax.ShapeDtypeStruct((n_rows, f), BF16),
        compiler_params=_params("arbitrary", "arbitrary"),
        name="moe_up",
    )(*route, xs, w_gate, w_up)

    y = pl.pallas_call(
        functools.partial(_moe_down_kernel, mi=mi),
        grid_spec=pltpu.PrefetchScalarGridSpec(
            num_scalar_prefetch=len(route),
            grid=(d // bn, n_tiles),
            in_specs=[pl.BlockSpec((bm, f), lambda j, i, *_: (i, 0)),
                      pl.BlockSpec(memory_space=pl.ANY)],
            out_specs=pl.BlockSpec((bm, bn), lambda j, i, *_: (i, j)),
            scratch_shapes=[pltpu.VMEM((f, bn), F32), pltpu.VMEM((f, bn), BF16), dma]),
        out_shape=jax.ShapeDtypeStruct((n_rows, d), F32),
        compiler_params=_params("arbitrary", "arbitrary"),
        name="moe_down",
    )(*route, hid, w_down)

    pos_tiles = pos.reshape(t // bt, bt, TOP_K).transpose(0, 2, 1).reshape(t // bt, 1, TOP_K * bt)
    return pl.pallas_call(
        _moe_combine_kernel,
        grid=(t // bt,),
        in_specs=[pl.BlockSpec((1, 1, TOP_K * bt), lambda i: (i, 0, 0), memory_space=pltpu.SMEM),
                  pl.BlockSpec((bt, d), lambda i: (i, 0)),
                  pl.BlockSpec((bt, LANES), lambda i: (i, 0)),
                  pl.BlockSpec(memory_space=pl.ANY)],
        out_specs=pl.BlockSpec((bt, d), lambda i: (i, 0)),
        out_shape=jax.ShapeDtypeStruct((t, d), F32),
        scratch_shapes=[pltpu.VMEM((TOP_K * bt, d), F32), pltpu.SemaphoreType.DMA(())],
        compiler_params=_params("arbitrary"),
        name="moe_combine",
    )(pos_tiles, xt, gates, y)


def _lambda_init(layer):
    return 0.8 - 0.6 * math.exp(-0.3 * layer)


def kernel(x, norm1_w, w_in, w_out, q_norm_w, k_norm_w, lambda_q1, lambda_k1, lambda_q2, lambda_k2,
           subln_w, conv_w, conv_b, dt_bias, a_log, d_skip, ssm_norm_w, norm2_w, ffn_w_gate, ffn_w_up,
           ffn_w_down, router_w, moe_w_gate, moe_w_up, moe_w_down):
    b, s, d = x.shape
    t = b * s
    depth = w_in.shape[0]
    ssm_width = ssm_norm_w.shape[1]
    conv_ch = conv_w.shape[2]
    ssm_heads = dt_bias.shape[1]
    att_width = (w_in.shape[2] - ssm_width - conv_ch - ssm_heads) // 3
    n_qk_heads = att_width // ATT_HEAD_DIM
    q_scale = ATT_HEAD_DIM ** -0.5 * math.log2(math.e)
    bn = 512
    z0 = 3 * att_width
    c0 = z0 + ssm_width
    nz, nx = ssm_width // bn, conv_ch // bn

    xt = x.reshape(t, d)
    w_in_t = jnp.swapaxes(w_in, 1, 2)
    for layer in range(depth):

        h = rmsnorm(xt, norm1_w[layer])
        gain = jnp.concatenate([jnp.tile(q_norm_w[layer] * q_scale, n_qk_heads),
                                jnp.tile(k_norm_w[layer], n_qk_heads)]).reshape(1, 2 * att_width)
        qk = matmul_ws(h, w_in_t, layer, 0, 2 * att_width, BF16, gain=gain, bn=bn, name="proj_qk")
        v = matmul_ws(h, w_in_t, layer, 2 * att_width, att_width, BF16, bn=bn, name="proj_v")
        cz = matmul_ws(h, w_in_t, layer, z0, ssm_width + conv_ch, F32, bn=bn, name="proj_ssm",
                       out_block=lambda j: jnp.where(j < nz, j + nx, j - nz))
        dt = matmul_narrow(h, w_in_t, layer, c0 + conv_ch, ssm_heads, name="proj_dt")

        lam_vecs = jnp.stack([lambda_q1[layer], lambda_k1[layer], lambda_q2[layer], lambda_k2[layer]])
        att = diff_attention(qk, v, lam_vecs, subln_w[layer], _lambda_init(layer), b, s)
        ssm = ssd_mixer(cz, dt, conv_w[layer], conv_b[layer], dt_bias[layer], a_log[layer],
                        d_skip[layer], ssm_norm_w[layer], b, s)
        xt = matmul_ws_res2(att, ssm, w_out, layer, xt, bn=bn)

        i = layer // 2
        if layer % 2 == 0:
            f = ffn_w_gate.shape[2]
            h2 = rmsnorm(xt, norm2_w[layer])
            hid = swiglu_up_ws(h2, ffn_w_gate, ffn_w_up, i)
            xt = matmul_residual_ktiled(hid, ffn_w_down[i].astype(BF16), xt, bk=f // 2, name="ffn_down")
        else:
            xt = moe_ffn(xt, norm2_w[layer], router_w[i], moe_w_gate, moe_w_up, moe_w_down, i)
    return xt.reshape(b, s, d)
```

```python
import functools
import math

import jax
import jax.numpy as jnp
from jax import lax
from jax.experimental import pallas as pl
from jax.experimental.pallas import tpu as pltpu

F32 = jnp.float32
BF16 = jnp.bfloat16
EPS = 1e-6
LANES = 128
VMEM_LIMIT = 56 * 1024 * 1024
VMEM_LIMIT_BIG = 60 * 1024 * 1024

ATT_HEAD_DIM = 128
SSM_HEAD_DIM = 64
SSM_GROUPS = 8
SSM_STATE = 128
CONV_K = 4
SSM_CHUNK = 128
TOP_K = 2
MOE_TILE = 512


def _params(*sem, vmem=VMEM_LIMIT):
    return pltpu.CompilerParams(dimension_semantics=sem, vmem_limit_bytes=vmem)


def _dot(a, b):
    return jnp.dot(a, b, preferred_element_type=F32)


def _dot_nt(a, b):
    return lax.dot_general(a, b, (((1,), (1,)), ((), ())), preferred_element_type=F32)


def _silu(x):
    return x / (1.0 + jnp.exp(-x))


def _split3(x):
    hi = x.astype(BF16)
    r1 = x - hi.astype(F32)
    mid = r1.astype(BF16)
    lo = (r1 - mid.astype(F32)).astype(BF16)
    return hi, mid, lo


def _rmsnorm_kernel(x_ref, w_ref, o_ref):
    x = x_ref[...]
    ms = jnp.mean(x * x, axis=-1, keepdims=True)
    o_ref[...] = (x * lax.rsqrt(ms + EPS) * w_ref[...]).astype(o_ref.dtype)


def rmsnorm(x, w, out_dtype=BF16, bt=256):
    t, d = x.shape
    bt = min(bt, t)
    return pl.pallas_call(
        _rmsnorm_kernel,
        grid=(t // bt,),
        in_specs=[pl.BlockSpec((bt, d), lambda i: (i, 0)),
                  pl.BlockSpec((1, d), lambda i: (0, 0))],
        out_specs=pl.BlockSpec((bt, d), lambda i: (i, 0)),
        out_shape=jax.ShapeDtypeStruct((t, d), out_dtype),
        compiler_params=_params("parallel"),
        name="rmsnorm",
    )(x, w.reshape(1, d))


def _cast_weight(w_ref, wb_sc):
    @pl.when(pl.program_id(1) == 0)
    def _():
        wb_sc[...] = w_ref[...].astype(BF16)


def _ws_kernel(x_ref, w_ref, o_ref, wb_sc, *, dot):
    _cast_weight(w_ref, wb_sc)
    o_ref[...] = dot(x_ref[...], wb_sc[...]).astype(o_ref.dtype)


def _ws_headnorm_kernel(x_ref, w_ref, g_ref, o_ref, wb_sc, *, dot):
    _cast_weight(w_ref, wb_sc)
    acc = dot(x_ref[...], wb_sc[...])
    for c in range(acc.shape[1] // ATT_HEAD_DIM):
        sl = slice(c * ATT_HEAD_DIM, (c + 1) * ATT_HEAD_DIM)
        a = acc[:, sl]
        ms = jnp.mean(a * a, axis=-1, keepdims=True)
        o_ref[:, sl] = (a * lax.rsqrt(ms + EPS) * g_ref[:, sl]).astype(o_ref.dtype)


def _ws_res2_kernel(x1_ref, x2_ref, w_ref, r_ref, o_ref, wb_sc):
    _cast_weight(w_ref, wb_sc)
    k1 = x1_ref.shape[1]
    o_ref[...] = (r_ref[...] + _dot(x1_ref[...], wb_sc[0:k1, :])) + _dot(x2_ref[...], wb_sc[k1:, :])


def _ws_swiglu_kernel(x_ref, wg_ref, wu_ref, o_ref, wgb_sc, wub_sc):
    _cast_weight(wg_ref, wgb_sc)
    _cast_weight(wu_ref, wub_sc)
    x = x_ref[...]
    g = _dot(x, wgb_sc[...])
    u = _dot(x, wub_sc[...])
    o_ref[...] = (_silu(g) * u).astype(o_ref.dtype)


def matmul_ws(x, wt, layer, col0, ncols, out_dtype, gain=None, out_block=None, bm=1024, bn=512, name="proj"):
    m, k = x.shape
    bm = min(bm, m)
    assert m % bm == 0 and ncols % bn == 0 and col0 % bn == 0
    j0 = col0 // bn
    ob = out_block if out_block is not None else (lambda j: j)
    in_specs = [pl.BlockSpec((bm, k), lambda j, i: (i, 0)),
                pl.BlockSpec((None, bn, k), lambda j, i: (layer, j0 + j, 0))]
    args = [x, wt]
    kern = _ws_kernel
    if gain is not None:
        in_specs.append(pl.BlockSpec((1, bn), lambda j, i: (0, j)))
        args.append(gain)
        kern = _ws_headnorm_kernel
    return pl.pallas_call(
        functools.partial(kern, dot=_dot_nt),
        grid=(ncols // bn, m // bm),
        in_specs=in_specs,
        out_specs=pl.BlockSpec((bm, bn), lambda j, i: (i, ob(j))),
        out_shape=jax.ShapeDtypeStruct((m, ncols), out_dtype),
        scratch_shapes=[pltpu.VMEM((bn, k), BF16)],
        compiler_params=_params("arbitrary", "arbitrary"),
        name=name,
    )(*args)


def matmul_ws_res2(x1, x2, w, layer, res, bm=1024, bn=512, name="out_proj"):
    m, k1 = x1.shape
    k2 = x2.shape[1]
    n = w.shape[2]
    bm = min(bm, m)
    assert m % bm == 0 and n % bn == 0 and w.shape[1] == k1 + k2
    return pl.pallas_call(
        _ws_res2_kernel,
        grid=(n // bn, m // bm),
        in_specs=[pl.BlockSpec((bm, k1), lambda j, i: (i, 0)),
                  pl.BlockSpec((bm, k2), lambda j, i: (i, 0)),
                  pl.BlockSpec((None, k1 + k2, bn), lambda j, i: (layer, 0, j)),
                  pl.BlockSpec((bm, bn), lambda j, i: (i, j))],
        out_specs=pl.BlockSpec((bm, bn), lambda j, i: (i, j)),
        out_shape=jax.ShapeDtypeStruct((m, n), F32),
        scratch_shapes=[pltpu.VMEM((k1 + k2, bn), BF16)],
        compiler_params=_params("arbitrary", "arbitrary"),
        name=name,
    )(x1, x2, w, res)


def swiglu_up_ws(x, wg, wu, layer, bm=1024, bn=256):
    m, k = x.shape
    n = wg.shape[2]
    bm = min(bm, m)
    assert m % bm == 0 and n % bn == 0
    return pl.pallas_call(
        _ws_swiglu_kernel,
        grid=(n // bn, m // bm),
        in_specs=[pl.BlockSpec((bm, k), lambda j, i: (i, 0)),
                  pl.BlockSpec((None, k, bn), lambda j, i: (layer, 0, j)),
                  pl.BlockSpec((None, k, bn), lambda j, i: (layer, 0, j))],
        out_specs=pl.BlockSpec((bm, bn), lambda j, i: (i, j)),
        out_shape=jax.ShapeDtypeStruct((m, n), BF16),
        scratch_shapes=[pltpu.VMEM((k, bn), BF16), pltpu.VMEM((k, bn), BF16)],
        compiler_params=_params("arbitrary", "arbitrary"),
        name="swiglu_up",
    )(x, wg, wu)


def _mm_res_ktiled_kernel(x_ref, w_ref, r_ref, o_ref):
    k = pl.program_id(2)

    @pl.when(k == 0)
    def _():
        o_ref[...] = r_ref[...] + _dot(x_ref[...], w_ref[...])

    @pl.when(k != 0)
    def _():
        o_ref[...] += _dot(x_ref[...], w_ref[...])


def _narrow_kernel(x_ref, w_ref, o_ref, wb_sc):
    @pl.when(pl.program_id(0) == 0)
    def _():
        wb_sc[...] = jnp.zeros(wb_sc.shape, BF16)
        wb_sc[0:w_ref.shape[0], :] = w_ref[...].astype(BF16)

    o_ref[...] = _dot_nt(x_ref[...], wb_sc[...])


def matmul_narrow(x, wt, layer, row0, nrows, bm=1024, name="proj_narrow"):
    m, k = x.shape
    bm = min(bm, m)
    assert m % bm == 0 and row0 % nrows == 0 and nrows % 16 == 0 and nrows <= LANES
    return pl.pallas_call(
        _narrow_kernel,
        grid=(m // bm,),
        in_specs=[pl.BlockSpec((bm, k), lambda i: (i, 0)),
                  pl.BlockSpec((None, nrows, k), lambda i: (layer, row0 // nrows, 0))],
        out_specs=pl.BlockSpec((bm, LANES), lambda i: (i, 0)),
        out_shape=jax.ShapeDtypeStruct((m, LANES), F32),
        scratch_shapes=[pltpu.VMEM((LANES, k), BF16)],
        compiler_params=_params("arbitrary"),
        name=name,
    )(x, wt)


def matmul_residual_ktiled(x, w, res, bk, bm=1024, bn=512, name="matmul_residual"):
    m, k = x.shape
    n = w.shape[1]
    bm, bn = min(bm, m), min(bn, n)
    assert m % bm == 0 and n % bn == 0 and k % bk == 0
    return pl.pallas_call(
        _mm_res_ktiled_kernel,
        grid=(m // bm, n // bn, k // bk),
        in_specs=[pl.BlockSpec((bm, bk), lambda i, j, kk: (i, kk)),
                  pl.BlockSpec((bk, bn), lambda i, j, kk: (kk, j)),
                  pl.BlockSpec((bm, bn), lambda i, j, kk: (i, j))],
        out_specs=pl.BlockSpec((bm, bn), lambda i, j, kk: (i, j)),
        out_shape=jax.ShapeDtypeStruct((m, n), F32),
        compiler_params=_params("parallel", "arbitrary", "arbitrary"),
        name=name,
    )(x, w, res)


def _attn_kernel(lam_ref, q_ref, k_ref, v_ref, w_ref, o_ref, m_sc, l_sc, acc_sc, *, blk, nsub, lam_init):
    d = ATT_HEAD_DIM
    nt = blk // LANES
    qi = pl.program_id(2)

    m_sc[...] = jnp.full(m_sc.shape, -jnp.inf, F32)
    l_sc[...] = jnp.zeros(l_sc.shape, F32)
    acc_sc[...] = jnp.zeros(acc_sc.shape, F32)

    row = lax.broadcasted_iota(jnp.int32, (blk, blk), 0)
    col = lax.broadcasted_iota(jnp.int32, (blk, blk), 1)
    keep = col <= row

    def chain(sub, c, k, v, masked):
        ch = sub * 2 + c
        q = q_ref[sub * blk:(sub + 1) * blk, c * d:(c + 1) * d]
        s = _dot_nt(q, k[:, c * d:(c + 1) * d])
        if masked:
            s = jnp.where(keep, s, -jnp.inf)
        tiles = [s[:, j * LANES:(j + 1) * LANES] for j in range(nt)]
        tile_max = functools.reduce(jnp.maximum, tiles)
        m_prev = m_sc[ch]
        m_new = jnp.maximum(m_prev, jnp.max(tile_max, axis=-1, keepdims=True))
        alpha = jnp.exp2(m_prev - m_new)
        ps = [jnp.exp2(tl - m_new) for tl in tiles]
        l_sc[ch] = alpha * l_sc[ch] + functools.reduce(jnp.add, ps)
        m_sc[ch] = m_new
        pv = _dot(jnp.concatenate(ps, axis=1).astype(BF16), v)
        for j in range(pv.shape[1] // LANES):
            sl = slice(j * LANES, (j + 1) * LANES)
            acc_sc[ch, :, sl] = alpha * acc_sc[ch, :, sl] + pv[:, sl]

    def step(kb, first_sub, diag_sub):
        start = pl.multiple_of(kb * blk, blk)
        k = k_ref[pl.ds(start, blk), :]
        v = v_ref[pl.ds(start, blk), :]
        for sub in range(first_sub, nsub):
            for c in range(2):
                chain(sub, c, k, v, sub == diag_sub)

    def body(kb, carry):
        step(kb, 0, None)
        return carry

    lax.fori_loop(0, qi * nsub, body, 0)
    for sub in range(nsub):
        step(qi * nsub + sub, sub, sub)

    lv = lam_ref[...]
    a1 = jnp.sum(lv[0:1] * lv[1:2], axis=-1, keepdims=True)
    a2 = jnp.sum(lv[2:3] * lv[3:4], axis=-1, keepdims=True)
    lam = jnp.exp(a1) - jnp.exp(a2) + lam_init
    gain = w_ref[...] * (1.0 - lam_init)
    for sub in range(nsub):
        l1 = jnp.sum(l_sc[2 * sub], axis=-1, keepdims=True)
        l2 = jnp.sum(l_sc[2 * sub + 1], axis=-1, keepdims=True)
        o = acc_sc[2 * sub] / l1 - lam * (acc_sc[2 * sub + 1] / l2)
        ms = jnp.mean(o * o, axis=-1, keepdims=True)
        o_ref[sub * blk:(sub + 1) * blk, :] = (o * lax.rsqrt(ms + EPS) * gain).astype(o_ref.dtype)


def diff_attention(qk, v, lam_vecs, subln_w, lam_init, batch, seq, blk=512, nsub=4):
    t, width = v.shape
    hv = 2 * ATT_HEAD_DIM
    heads = width // hv
    blk = min(blk, seq)
    nsub = min(nsub, seq // blk)
    bq = blk * nsub
    nq = seq // bq
    kern = functools.partial(_attn_kernel, blk=blk, nsub=nsub, lam_init=lam_init)
    return pl.pallas_call(
        kern,
        grid=(batch, heads, nq),
        in_specs=[pl.BlockSpec((4, ATT_HEAD_DIM), lambda b, h, i: (0, 0)),
                  pl.BlockSpec((bq, hv), lambda b, h, i: (b * nq + i, h)),
                  pl.BlockSpec((seq, hv), lambda b, h, i: (b, heads + h)),
                  pl.BlockSpec((seq, hv), lambda b, h, i: (b, h)),
                  pl.BlockSpec((1, hv), lambda b, h, i: (0, 0))],
        out_specs=pl.BlockSpec((bq, hv), lambda b, h, i: (b * nq + i, h)),
        out_shape=jax.ShapeDtypeStruct((t, width), BF16),
        scratch_shapes=[pltpu.VMEM((2 * nsub, blk, LANES), F32),
                        pltpu.VMEM((2 * nsub, blk, LANES), F32),
                        pltpu.VMEM((2 * nsub, blk, hv), F32)],
        compiler_params=_params("parallel", "parallel", "arbitrary"),
        name="diff_attention",
    )(lam_vecs, qk, qk, v, subln_w.reshape(1, hv))


def _ssd_kernel(xbc_ref, z_ref, dt_ref, cw_ref, cb_ref, dtb_ref, alog_ref, dsk_ref, nw_ref,
                o_ref, ext_sc, state_sc, *, width):
    L = SSM_CHUNK
    n = SSM_STATE
    gw = width // SSM_GROUPS
    hpg = gw // SSM_HEAD_DIM
    pad = 8
    c = pl.program_id(1)

    @pl.when(c == 0)
    def _():
        ext_sc[0:pad, :] = jnp.zeros((pad, ext_sc.shape[1]), F32)
        state_sc[...] = jnp.zeros(state_sc.shape, F32)

    ext_sc[pad:pad + L, :] = xbc_ref[...]

    def conv_silu(lo, hi):
        a = ext_sc[:, lo:hi]
        acc = cb_ref[:, lo:hi] + a[pad:pad + L] * cw_ref[CONV_K - 1:CONV_K, lo:hi]
        for tap in range(CONV_K - 1):
            shifted = pltpu.roll(a, CONV_K - 1 - tap, axis=0)[pad:pad + L]
            acc = acc + shifted * cw_ref[tap:tap + 1, lo:hi]
        return _silu(acc)

    dtr = dt_ref[...] + dtb_ref[...]
    dtp = jnp.maximum(dtr, 0.0) + jnp.log1p(jnp.exp(-jnp.abs(dtr)))
    a = -jnp.exp(alog_ref[...])
    d_a = dtp * a
    row = lax.broadcasted_iota(jnp.int32, (L, L), 0)
    col = lax.broadcasted_iota(jnp.int32, (L, L), 1)
    causal = col <= row
    tril = jnp.where(causal, 1.0, 0.0).astype(BF16)
    hi_, mid_, lo_ = _split3(d_a)
    acum = _dot(tril, hi_) + _dot(tril, mid_) + _dot(tril, lo_)
    acum_t = acum.T
    dtp_t = dtp.T
    e_acum = jnp.exp(acum)
    a_last = acum[L - 1:L, :]
    to_end = jnp.exp(a_last - acum) * dtp
    e_last = jnp.exp(a_last)

    head_of_col = lax.broadcasted_iota(jnp.int32, (1, gw), 1) // SSM_HEAD_DIM

    def expand(x, g):
        out = x[:, g * hpg:g * hpg + 1]
        for hd in range(1, hpg):
            out = jnp.where(head_of_col == hd, x[:, g * hpg + hd:g * hpg + hd + 1], out)
        return jnp.broadcast_to(out, (x.shape[0], gw))

    for g in range(SSM_GROUPS):
        xg = conv_silu(g * gw, (g + 1) * gw)
        bg = conv_silu(width + g * n, width + (g + 1) * n)
        cg = conv_silu(width + SSM_GROUPS * n + g * n, width + SSM_GROUPS * n + (g + 1) * n)
        xg16 = xg.astype(BF16)
        cg16 = cg.astype(BF16)
        cb = _dot_nt(cg16, bg.astype(BF16))
        y = None
        for hd in range(hpg):
            h = g * hpg + hd
            seg = acum[:, h:h + 1] - acum_t[h:h + 1, :]
            decay = jnp.exp(jnp.where(causal, seg, -jnp.inf))
            w = cb * decay * dtp_t[h:h + 1, :]
            yh = _dot(w.astype(BF16), xg16)
            y = yh if y is None else jnp.where(head_of_col == hd, yh, y)
        st = state_sc[g]
        y = y + _dot(cg16, st.astype(BF16)) * expand(e_acum, g)
        xw = (xg * expand(to_end, g)).astype(BF16)
        state_sc[g] = st * expand(e_last, g) + _dot(bg.T.astype(BF16), xw)
        y = y + dsk_ref[:, g * gw:(g + 1) * gw] * xg
        gated = y * _silu(z_ref[:, g * gw:(g + 1) * gw])
        ms = jnp.mean(gated * gated, axis=-1, keepdims=True)
        o_ref[:, g * gw:(g + 1) * gw] = (gated * lax.rsqrt(ms + EPS)
                                         * nw_ref[:, g * gw:(g + 1) * gw]).astype(o_ref.dtype)

    ext_sc[0:pad, :] = xbc_ref[L - pad:L, :]


def ssd_mixer(cz, dt, conv_w, conv_b, dt_bias, a_log, d_skip, norm_w, batch, seq):
    t = cz.shape[0]
    conv_ch = conv_w.shape[1]
    width = norm_w.shape[0]
    heads = width // SSM_HEAD_DIM
    L = SSM_CHUNK
    nc = seq // L
    assert cz.shape[1] == conv_ch + width and conv_ch % width == 0
    z_blk = conv_ch // width
    pad_heads = lambda v: jnp.pad(v.reshape(1, heads), ((0, 0), (0, LANES - heads)))
    dsk = jnp.repeat(d_skip, SSM_HEAD_DIM).reshape(1, width)
    return pl.pallas_call(
        functools.partial(_ssd_kernel, width=width),
        grid=(batch, nc),
        in_specs=[pl.BlockSpec((L, conv_ch), lambda b, c: (b * nc + c, 0)),
                  pl.BlockSpec((L, width), lambda b, c: (b * nc + c, z_blk)),
                  pl.BlockSpec((L, LANES), lambda b, c: (b * nc + c, 0)),
                  pl.BlockSpec((CONV_K, conv_ch), lambda b, c: (0, 0)),
                  pl.BlockSpec((1, conv_ch), lambda b, c: (0, 0)),
                  pl.BlockSpec((1, LANES), lambda b, c: (0, 0)),
                  pl.BlockSpec((1, LANES), lambda b, c: (0, 0)),
                  pl.BlockSpec((1, width), lambda b, c: (0, 0)),
                  pl.BlockSpec((1, width), lambda b, c: (0, 0))],
        out_specs=pl.BlockSpec((L, width), lambda b, c: (b * nc + c, 0)),
        out_shape=jax.ShapeDtypeStruct((t, width), BF16),
        scratch_shapes=[pltpu.VMEM((8 + L, conv_ch), F32),
                        pltpu.VMEM((SSM_GROUPS, SSM_STATE, width // SSM_GROUPS), F32)],
        compiler_params=_params("parallel", "arbitrary"),
        name="ssd_mixer",
    )(cz, cz, dt, conv_w, conv_b.reshape(1, conv_ch), pad_heads(dt_bias), pad_heads(a_log), dsk,
      norm_w.reshape(1, width))


def _norm_router_kernel(x_ref, w_ref, rw_ref, h_ref, idx_ref, gate_ref, *, n_experts):
    x = x_ref[...]
    ms = jnp.mean(x * x, axis=-1, keepdims=True)
    hn = x * lax.rsqrt(ms + EPS) * w_ref[...]
    half = hn.shape[1] // 2
    hb = hn.astype(BF16).astype(F32)
    hi = pltpu.bitcast(hb[:, :half], jnp.uint32)
    lo = pltpu.bitcast(hb[:, half:], jnp.uint32)
    h_ref[...] = hi | (lo >> 16)
    hh, hm, _ = _split3(hn)
    rh, rm, _ = _split3(rw_ref[...])
    logits = _dot(hh, rh) + _dot(hh, rm) + _dot(hm, rh)
    lane = lax.broadcasted_iota(jnp.int32, logits.shape, 1)
    lanef = lane.astype(F32)
    logits = jnp.where(lane < n_experts, logits, -jnp.inf)
    m1 = jnp.max(logits, axis=-1, keepdims=True)
    i1 = jnp.min(jnp.where(logits == m1, lanef, float(LANES)), axis=-1, keepdims=True)
    rest = jnp.where(lanef == i1, -jnp.inf, logits)
    m2 = jnp.max(rest, axis=-1, keepdims=True)
    i2 = jnp.min(jnp.where(rest == m2, lanef, float(LANES)), axis=-1, keepdims=True)
    e = jnp.exp(m2 - m1)
    g1 = 1.0 / (1.0 + e)
    idx_ref[...] = jnp.where(lane == 0, i1, i2).astype(jnp.int32)
    gate_ref[...] = jnp.where(lane == 0, g1, e * g1)


def norm_router(x, w, router_w, bt=256):
    t, d = x.shape
    n_experts = router_w.shape[1]
    bt = min(bt, t)
    rw = jnp.pad(router_w, ((0, 0), (0, LANES - n_experts)))
    return pl.pallas_call(
        functools.partial(_norm_router_kernel, n_experts=n_experts),
        grid=(t // bt,),
        in_specs=[pl.BlockSpec((bt, d), lambda i: (i, 0)),
                  pl.BlockSpec((1, d), lambda i: (0, 0)),
                  pl.BlockSpec((d, LANES), lambda i: (0, 0))],
        out_specs=[pl.BlockSpec((bt, d // 2), lambda i: (i, 0)),
                   pl.BlockSpec((bt, LANES), lambda i: (i, 0)),
                   pl.BlockSpec((bt, LANES), lambda i: (i, 0))],
        out_shape=[jax.ShapeDtypeStruct((t, d // 2), jnp.uint32),
                   jax.ShapeDtypeStruct((t, LANES), jnp.int32),
                   jax.ShapeDtypeStruct((t, LANES), F32)],
        compiler_params=_params("parallel"),
        name="norm_router",
    )(x, w.reshape(1, d), rw)


def _issue_row_gather(idx_ref, n_rows, src_hbm, dst_ref, sem):
    for r in range(n_rows):
        pltpu.make_async_copy(src_hbm.at[pl.ds(idx_ref[0, 0, r], 1), :],
                              dst_ref.at[pl.ds(r, 1), :], sem).start()


def _wait_row_gather(n_rows, src_hbm, dst_ref, sem):
    pltpu.make_async_copy(src_hbm.at[pl.ds(0, n_rows), :], dst_ref, sem).wait()


def _moe_gather_kernel(tv_ref, tok_ref, tok_next_ref, h_hbm, o_ref, rows_sc, sems):
    i = pl.program_id(0)
    n = pl.num_programs(0)
    bm = o_ref.shape[0]
    slot = lax.rem(i, 2)

    @pl.when(jnp.logical_and(i == 0, tv_ref[0] != 0))
    def _():
        _issue_row_gather(tok_ref, bm, h_hbm, rows_sc.at[0], sems.at[0])

    nxt = jnp.minimum(i + 1, n - 1)

    @pl.when(jnp.logical_and(i + 1 < n, tv_ref[nxt] != 0))
    def _():
        _issue_row_gather(tok_next_ref, bm, h_hbm, rows_sc.at[1 - slot], sems.at[1 - slot])

    @pl.when(tv_ref[i] != 0)
    def _():
        _wait_row_gather(bm, h_hbm, rows_sc.at[slot], sems.at[slot])
        words = rows_sc[slot]
        half = words.shape[1]
        o_ref[:, :half] = pltpu.bitcast(words & jnp.uint32(0xFFFF0000), F32).astype(o_ref.dtype)
        o_ref[:, half:] = pltpu.bitcast(words << 16, F32).astype(o_ref.dtype)

    @pl.when(tv_ref[i] == 0)
    def _():
        o_ref[...] = jnp.zeros(o_ref.shape, o_ref.dtype)


def _expert_weights(te_ref, ci_ref, ne_ref, nc_ref, mi, bn, w_hbms, wf_scs, wb_scs, sems):
    j = pl.program_id(0)
    i = pl.program_id(1)
    nj = pl.num_programs(0)
    changed = jnp.logical_or(i == 0, te_ref[i] != te_ref[jnp.maximum(i - 1, 0)])

    def copies(e, jj):
        col = pl.multiple_of(jj * bn, bn)
        return [pltpu.make_async_copy(w.at[mi, e, :, pl.ds(col, bn)], wf, sem)
                for w, wf, sem in zip(w_hbms, wf_scs, sems)]

    @pl.when(changed)
    def _():
        @pl.when(j * nc_ref[0] + ci_ref[i] == 0)
        def _():
            for cp in copies(te_ref[0], 0):
                cp.start()

        for cp in copies(te_ref[i], j):
            cp.wait()
        for wf, wb in zip(wf_scs, wb_scs):
            wb[...] = wf[...].astype(BF16)

        more_here = ne_ref[i] >= 0

        @pl.when(more_here)
        def _():
            for cp in copies(ne_ref[i], j):
                cp.start()

        @pl.when(jnp.logical_and(jnp.logical_not(more_here), j + 1 < nj))
        def _():
            for cp in copies(te_ref[0], j + 1):
                cp.start()


def _moe_up_kernel(te_ref, tv_ref, ci_ref, ne_ref, nc_ref, x_ref, wg_hbm, wu_hbm, o_ref,
                   wgf_sc, wuf_sc, wgb_sc, wub_sc, sem_g, sem_u, *, mi):
    i = pl.program_id(1)
    valid = tv_ref[i] != 0
    _expert_weights(te_ref, ci_ref, ne_ref, nc_ref, mi, o_ref.shape[1], (wg_hbm, wu_hbm),
                    (wgf_sc, wuf_sc), (wgb_sc, wub_sc), (sem_g, sem_u))

    @pl.when(valid)
    def _():
        x = x_ref[...]
        g = _dot(x, wgb_sc[...])
        u = _dot(x, wub_sc[...])
        o_ref[...] = (_silu(g) * u).astype(o_ref.dtype)

    @pl.when(jnp.logical_not(valid))
    def _():
        o_ref[...] = jnp.zeros(o_ref.shape, o_ref.dtype)


def _moe_down_kernel(te_ref, tv_ref, ci_ref, ne_ref, nc_ref, h_ref, wd_hbm, o_ref,
                     wdf_sc, wdb_sc, sem_d, *, mi):
    i = pl.program_id(1)
    valid = tv_ref[i] != 0
    _expert_weights(te_ref, ci_ref, ne_ref, nc_ref, mi, o_ref.shape[1], (wd_hbm,),
                    (wdf_sc,), (wdb_sc,), (sem_d,))

    @pl.when(valid)
    def _():
        o_ref[...] = _dot(h_ref[...], wdb_sc[...])

    @pl.when(jnp.logical_not(valid))
    def _():
        o_ref[...] = jnp.zeros(o_ref.shape, o_ref.dtype)


def _moe_combine_kernel(pos_ref, x_ref, g_ref, y_hbm, o_ref, rows_sc, sem):
    bt = x_ref.shape[0]
    _issue_row_gather(pos_ref, TOP_K * bt, y_hbm, rows_sc, sem)
    _wait_row_gather(TOP_K * bt, y_hbm, rows_sc, sem)
    g = g_ref[...]
    o_ref[...] = x_ref[...] + (g[:, 0:1] * rows_sc[0:bt, :] + g[:, 1:2] * rows_sc[bt:2 * bt, :])


def moe_ffn(xt, norm_w, router_w, w_gate, w_up, w_down, mi, bn=512, bn_up=512, bt=256):
    t, d = xt.shape
    _, n_experts, _, f = w_gate.shape
    bm = min(MOE_TILE, t)
    bt = min(bt, t)
    h, idx, gates = norm_router(xt, norm_w, router_w)

    e_flat = idx[:, :TOP_K].reshape(-1)
    onehot = (e_flat[:, None] == jnp.arange(n_experts, dtype=jnp.int32)[None, :]).astype(jnp.int32)
    csum = jnp.cumsum(onehot, axis=0)
    rank = jnp.sum((csum - onehot) * onehot, axis=1)
    counts = csum[-1]
    tiles_e = (counts + bm - 1) // bm
    tile_end = jnp.cumsum(tiles_e)
    row_start = (tile_end - tiles_e) * bm
    pos = row_start[e_flat] + rank
    n_tiles = (TOP_K * t) // bm + n_experts
    n_rows = n_tiles * bm
    row_token = jnp.zeros((n_rows,), jnp.int32).at[pos].set(jnp.arange(TOP_K * t, dtype=jnp.int32) // TOP_K)
    tile_ids = jnp.arange(n_tiles, dtype=jnp.int32)
    n_valid = tile_end[-1]
    tile_valid = (tile_ids < n_valid).astype(jnp.int32)
    last_valid = jnp.minimum(tile_ids, n_valid - 1)
    tile_expert = jnp.sum((last_valid[:, None] >= tile_end[None, :]).astype(jnp.int32), axis=1)
    tile_expert = jnp.minimum(tile_expert, n_experts - 1)

    tok_tiles = row_token.reshape(n_tiles, 1, bm)
    xs = pl.pallas_call(
        _moe_gather_kernel,
        grid_spec=pltpu.PrefetchScalarGridSpec(
            num_scalar_prefetch=1,
            grid=(n_tiles,),
            in_specs=[pl.BlockSpec((1, 1, bm), lambda i, tv: (i, 0, 0), memory_space=pltpu.SMEM),
                      pl.BlockSpec((1, 1, bm), lambda i, tv: (jnp.minimum(i + 1, n_tiles - 1), 0, 0),
                                   memory_space=pltpu.SMEM),
                      pl.BlockSpec(memory_space=pl.ANY)],
            out_specs=pl.BlockSpec((bm, d), lambda i, tv: (i, 0)),
            scratch_shapes=[pltpu.VMEM((2, bm, d // 2), jnp.uint32), pltpu.SemaphoreType.DMA((2,))]),
        out_shape=jax.ShapeDtypeStruct((n_rows, d), BF16),
        compiler_params=_params("arbitrary"),
        name="moe_gather",
    )(tile_valid, tok_tiles, tok_tiles, h)

    owns = tiles_e > 0
    eids = jnp.arange(n_experts, dtype=jnp.int32)
    later = jnp.logical_and(owns[None, :], eids[None, :] > eids[:, None])
    next_owner = jnp.min(jnp.where(later, eids[None, :], n_experts), axis=1)
    next_owner = jnp.where(next_owner == n_experts, -1, next_owner).astype(jnp.int32)
    owner_idx = (jnp.cumsum(owns.astype(jnp.int32)) - owns.astype(jnp.int32)).astype(jnp.int32)
    tile_ci = owner_idx[tile_expert]
    tile_ne = next_owner[tile_expert]
    n_owners = jnp.sum(owns.astype(jnp.int32)).reshape(1)
    route = (tile_expert, tile_valid, tile_ci, tile_ne, n_owners)
    dma = pltpu.SemaphoreType.DMA(())

    hid = pl.pallas_call(
        functools.partial(_moe_up_kernel, mi=mi),
        grid_spec=pltpu.PrefetchScalarGridSpec(
            num_scalar_prefetch=len(route),
            grid=(f // bn_up, n_tiles),
            in_specs=[pl.BlockSpec((bm, d), lambda j, i, *_: (i, 0)),
                      pl.BlockSpec(memory_space=pl.ANY),
                      pl.BlockSpec(memory_space=pl.ANY)],
            out_specs=pl.BlockSpec((bm, bn_up), lambda j, i, *_: (i, j)),
            scratch_shapes=[pltpu.VMEM((d, bn_up), F32), pltpu.VMEM((d, bn_up), F32),
                            pltpu.VMEM((d, bn_up), BF16), pltpu.VMEM((d, bn_up), BF16), dma, dma]),
        out_shape=jax.ShapeDtypeStruct((n_rows, f), BF16),
        compiler_params=_params("arbitrary", "arbitrary"),
        name="moe_up",
    )(*route, xs, w_gate, w_up)

    y = pl.pallas_call(
        functools.partial(_moe_down_kernel, mi=mi),
        grid_spec=pltpu.PrefetchScalarGridSpec(
            num_scalar_prefetch=len(route),
            grid=(d // bn, n_tiles),
            in_specs=[pl.BlockSpec((bm, f), lambda j, i, *_: (i, 0)),
                      pl.BlockSpec(memory_space=pl.ANY)],
            out_specs=pl.BlockSpec((bm, bn), lambda j, i, *_: (i, j)),
            scratch_shapes=[pltpu.VMEM((f, bn), F32), pltpu.VMEM((f, bn), BF16), dma]),
        out_shape=jax.ShapeDtypeStruct((n_rows, d), F32),
        compiler_params=_params("arbitrary", "arbitrary"),
        name="moe_down",
    )(*route, hid, w_down)

    pos_tiles = pos.reshape(t // bt, bt, TOP_K).transpose(0, 2, 1).reshape(t // bt, 1, TOP_K * bt)
    return pl.pallas_call(
        _moe_combine_kernel,
        grid=(t // bt,),
        in_specs=[pl.BlockSpec((1, 1, TOP_K * bt), lambda i: (i, 0, 0), memory_space=pltpu.SMEM),
                  pl.BlockSpec((bt, d), lambda i: (i, 0)),
                  pl.BlockSpec((bt, LANES), lambda i: (i, 0)),
                  pl.BlockSpec(memory_space=pl.ANY)],
        out_specs=pl.BlockSpec((bt, d), lambda i: (i, 0)),
        out_shape=jax.ShapeDtypeStruct((t, d), F32),
        scratch_shapes=[pltpu.VMEM((TOP_K * bt, d), F32), pltpu.SemaphoreType.DMA(())],
        compiler_params=_params("arbitrary"),
        name="moe_combine",
    )(pos_tiles, xt, gates, y)


def _lambda_init(layer):
    return 0.8 - 0.6 * math.exp(-0.3 * layer)


def kernel(x, norm1_w, w_in, w_out, q_norm_w, k_norm_w, lambda_q1, lambda_k1, lambda_q2, lambda_k2,
           subln_w, conv_w, conv_b, dt_bias, a_log, d_skip, ssm_norm_w, norm2_w, ffn_w_gate, ffn_w_up,
           ffn_w_down, router_w, moe_w_gate, moe_w_up, moe_w_down):
    b, s, d = x.shape
    t = b * s
    depth = w_in.shape[0]
    ssm_width = ssm_norm_w.shape[1]
    conv_ch = conv_w.shape[2]
    ssm_heads = dt_bias.shape[1]
    att_width = (w_in.shape[2] - ssm_width - conv_ch - ssm_heads) // 3
    n_qk_heads = att_width // ATT_HEAD_DIM
    q_scale = ATT_HEAD_DIM ** -0.5 * math.log2(math.e)
    bn = 512
    z0 = 3 * att_width
    c0 = z0 + ssm_width
    nz, nx = ssm_width // bn, conv_ch // bn

    xt = x.reshape(t, d)
    w_in_t = jnp.swapaxes(w_in, 1, 2)
    for layer in range(depth):

        h = rmsnorm(xt, norm1_w[layer])
        gain = jnp.concatenate([jnp.tile(q_norm_w[layer] * q_scale, n_qk_heads),
                                jnp.tile(k_norm_w[layer], n_qk_heads)]).reshape(1, 2 * att_width)
        qk = matmul_ws(h, w_in_t, layer, 0, 2 * att_width, BF16, gain=gain, bn=bn, name="proj_qk")
        v = matmul_ws(h, w_in_t, layer, 2 * att_width, att_width, BF16, bn=bn, name="proj_v")
        cz = matmul_ws(h, w_in_t, layer, z0, ssm_width + conv_ch, F32, bn=bn, name="proj_ssm",
                       out_block=lambda j: jnp.where(j < nz, j + nx, j - nz))
        dt = matmul_narrow(h, w_in_t, layer, c0 + conv_ch, ssm_heads, name="proj_dt")

        lam_vecs = jnp.stack([lambda_q1[layer], lambda_k1[layer], lambda_q2[layer], lambda_k2[layer]])
        att = diff_attention(qk, v, lam_vecs, subln_w[layer], _lambda_init(layer), b, s)
        ssm = ssd_mixer(cz, dt, conv_w[layer], conv_b[layer], dt_bias[layer], a_log[layer],
                        d_skip[layer], ssm_norm_w[layer], b, s)
        xt = matmul_ws_res2(att, ssm, w_out, layer, xt, bn=bn)

        i = layer // 2
        if layer % 2 == 0:
            f = ffn_w_gate.shape[2]
            h2 = rmsnorm(xt, norm2_w[layer])
            hid = swiglu_up_ws(h2, ffn_w_gate, ffn_w_up, i)
            xt = matmul_residual_ktiled(hid, ffn_w_down[i].astype(BF16), xt, bk=f // 2, name="ffn_down")
        else:
            xt = moe_ffn(xt, norm2_w[layer], router_w[i], moe_w_gate, moe_w_up, moe_w_down, i)
    return xt.reshape(b, s, d)
```

```python
import functools
import math

import jax
import jax.numpy as jnp
from jax import lax
from jax.experimental import pallas as pl
from jax.experimental.pallas import tpu as pltpu

F32 = jnp.float32
BF16 = jnp.bfloat16
EPS = 1e-6
LANES = 128
VMEM_LIMIT = 56 * 1024 * 1024
VMEM_LIMIT_BIG = 60 * 1024 * 1024

ATT_HEAD_DIM = 128
SSM_HEAD_DIM = 64
SSM_GROUPS = 8
SSM_STATE = 128
CONV_K = 4
SSM_CHUNK = 128
TOP_K = 2
MOE_TILE = 512


def _params(*sem, vmem=VMEM_LIMIT):
    return pltpu.CompilerParams(dimension_semantics=sem, vmem_limit_bytes=vmem)


def _dot(a, b):
    return jnp.dot(a, b, preferred_element_type=F32)


def _dot_nt(a, b):
    return lax.dot_general(a, b, (((1,), (1,)), ((), ())), preferred_element_type=F32)


def _silu(x):
    return x / (1.0 + jnp.exp(-x))


def _split3(x):
    hi = x.astype(BF16)
    r1 = x - hi.astype(F32)
    mid = r1.astype(BF16)
    lo = (r1 - mid.astype(F32)).astype(BF16)
    return hi, mid, lo


def _rmsnorm_kernel(x_ref, w_ref, o_ref):
    x = x_ref[...]
    ms = jnp.mean(x * x, axis=-1, keepdims=True)
    o_ref[...] = (x * lax.rsqrt(ms + EPS) * w_ref[...]).astype(o_ref.dtype)


def rmsnorm(x, w, out_dtype=BF16, bt=256):
    t, d = x.shape
    bt = min(bt, t)
    return pl.pallas_call(
        _rmsnorm_kernel,
        grid=(t // bt,),
        in_specs=[pl.BlockSpec((bt, d), lambda i: (i, 0)),
                  pl.BlockSpec((1, d), lambda i: (0, 0))],
        out_specs=pl.BlockSpec((bt, d), lambda i: (i, 0)),
        out_shape=jax.ShapeDtypeStruct((t, d), out_dtype),
        compiler_params=_params("parallel"),
        name="rmsnorm",
    )(x, w.reshape(1, d))


def _cast_weight(w_ref, wb_sc):
    @pl.when(pl.program_id(1) == 0)
    def _():
        wb_sc[...] = w_ref[...].astype(BF16)


def _ws_kernel(x_ref, w_ref, o_ref, wb_sc, *, dot):
    _cast_weight(w_ref, wb_sc)
    o_ref[...] = dot(x_ref[...], wb_sc[...]).astype(o_ref.dtype)


def _ws_headnorm_kernel(x_ref, w_ref, g_ref, o_ref, wb_sc, *, dot):
    _cast_weight(w_ref, wb_sc)
    acc = dot(x_ref[...], wb_sc[...])
    for c in range(acc.shape[1] // ATT_HEAD_DIM):
        sl = slice(c * ATT_HEAD_DIM, (c + 1) * ATT_HEAD_DIM)
        a = acc[:, sl]
        ms = jnp.mean(a * a, axis=-1, keepdims=True)
        o_ref[:, sl] = (a * lax.rsqrt(ms + EPS) * g_ref[:, sl]).astype(o_ref.dtype)


def _ws_res2_kernel(x1_ref, x2_ref, w_ref, r_ref, o_ref, wb_sc):
    _cast_weight(w_ref, wb_sc)
    k1 = x1_ref.shape[1]
    o_ref[...] = (r_ref[...] + _dot(x1_ref[...], wb_sc[0:k1, :])) + _dot(x2_ref[...], wb_sc[k1:, :])


def _ws_swiglu_kernel(x_ref, wg_ref, wu_ref, o_ref, wgb_sc, wub_sc):
    _cast_weight(wg_ref, wgb_sc)
    _cast_weight(wu_ref, wub_sc)
    x = x_ref[...]
    g = _dot(x, wgb_sc[...])
    u = _dot(x, wub_sc[...])
    o_ref[...] = (_silu(g) * u).astype(o_ref.dtype)


def matmul_ws(x, wt, layer, col0, ncols, out_dtype, gain=None, out_block=None, bm=1024, bn=512, name="proj"):
    m, k = x.shape
    bm = min(bm, m)
    assert m % bm == 0 and ncols % bn == 0 and col0 % bn == 0
    j0 = col0 // bn
    ob = out_block if out_block is not None else (lambda j: j)
    in_specs = [pl.BlockSpec((bm, k), lambda j, i: (i, 0)),
                pl.BlockSpec((None, bn, k), lambda j, i: (layer, j0 + j, 0))]
    args = [x, wt]
    kern = _ws_kernel
    if gain is not None:
        in_specs.append(pl.BlockSpec((1, bn), lambda j, i: (0, j)))
        args.append(gain)
        kern = _ws_headnorm_kernel
    return pl.pallas_call(
        functools.partial(kern, dot=_dot_nt),
        grid=(ncols // bn, m // bm),
        in_specs=in_specs,
        out_specs=pl.BlockSpec((bm, bn), lambda j, i: (i, ob(j))),
        out_shape=jax.ShapeDtypeStruct((m, ncols), out_dtype),
        scratch_shapes=[pltpu.VMEM((bn, k), BF16)],
        compiler_params=_params("arbitrary", "arbitrary"),
        name=name,
    )(*args)


def matmul_ws_res2(x1, x2, w, layer, res, bm=1024, bn=512, name="out_proj"):
    m, k1 = x1.shape
    k2 = x2.shape[1]
    n = w.shape[2]
    bm = min(bm, m)
    assert m % bm == 0 and n % bn == 0 and w.shape[1] == k1 + k2
    return pl.pallas_call(
        _ws_res2_kernel,
        grid=(n // bn, m // bm),
        in_specs=[pl.BlockSpec((bm, k1), lambda j, i: (i, 0)),
                  pl.BlockSpec((bm, k2), lambda j, i: (i, 0)),
                  pl.BlockSpec((None, k1 + k2, bn), lambda j, i: (layer, 0, j)),
                  pl.BlockSpec((bm, bn), lambda j, i: (i, j))],
        out_specs=pl.BlockSpec((bm, bn), lambda j, i: (i, j)),
        out_shape=jax.ShapeDtypeStruct((m, n), F32),
        scratch_shapes=[pltpu.VMEM((k1 + k2, bn), BF16)],
        compiler_params=_params("arbitrary", "arbitrary"),
        name=name,
    )(x1, x2, w, res)


def swiglu_up_ws(x, wg, wu, layer, bm=1024, bn=256):
    m, k = x.shape
    n = wg.shape[2]
    bm = min(bm, m)
    assert m % bm == 0 and n % bn == 0
    return pl.pallas_call(
        _ws_swiglu_kernel,
        grid=(n // bn, m // bm),
        in_specs=[pl.BlockSpec((bm, k), lambda j, i: (i, 0)),
                  pl.BlockSpec((None, k, bn), lambda j, i: (layer, 0, j)),
                  pl.BlockSpec((None, k, bn), lambda j, i: (layer, 0, j))],
        out_specs=pl.BlockSpec((bm, bn), lambda j, i: (i, j)),
        out_shape=jax.ShapeDtypeStruct((m, n), BF16),
        scratch_shapes=[pltpu.VMEM((k, bn), BF16), pltpu.VMEM((k, bn), BF16)],
        compiler_params=_params("arbitrary", "arbitrary"),
        name="swiglu_up",
    )(x, wg, wu)


def _mm_res_ktiled_kernel(x_ref, w_ref, r_ref, o_ref):
    k = pl.program_id(2)

    @pl.when(k == 0)
    def _():
        o_ref[...] = r_ref[...] + _dot(x_ref[...], w_ref[...])

    @pl.when(k != 0)
    def _():
        o_ref[...] += _dot(x_ref[...], w_ref[...])


def _narrow_kernel(x_ref, w_ref, o_ref, wb_sc):
    @pl.when(pl.program_id(0) == 0)
    def _():
        wb_sc[...] = jnp.zeros(wb_sc.shape, BF16)
        wb_sc[0:w_ref.shape[0], :] = w_ref[...].astype(BF16)

    o_ref[...] = _dot_nt(x_ref[...], wb_sc[...])


def matmul_narrow(x, wt, layer, row0, nrows, bm=1024, name="proj_narrow"):
    m, k = x.shape
    bm = min(bm, m)
    assert m % bm == 0 and row0 % nrows == 0 and nrows % 16 == 0 and nrows <= LANES
    return pl.pallas_call(
        _narrow_kernel,
        grid=(m // bm,),
        in_specs=[pl.BlockSpec((bm, k), lambda i: (i, 0)),
                  pl.BlockSpec((None, nrows, k), lambda i: (layer, row0 // nrows, 0))],
        out_specs=pl.BlockSpec((bm, LANES), lambda i: (i, 0)),
        out_shape=jax.ShapeDtypeStruct((m, LANES), F32),
        scratch_shapes=[pltpu.VMEM((LANES, k), BF16)],
        compiler_params=_params("arbitrary"),
        name=name,
    )(x, wt)


def matmul_residual_ktiled(x, w, res, bk, bm=1024, bn=512, name="matmul_residual"):
    m, k = x.shape
    n = w.shape[1]
    bm, bn = min(bm, m), min(bn, n)
    assert m % bm == 0 and n % bn == 0 and k % bk == 0
    return pl.pallas_call(
        _mm_res_ktiled_kernel,
        grid=(m // bm, n // bn, k // bk),
        in_specs=[pl.BlockSpec((bm, bk), lambda i, j, kk: (i, kk)),
                  pl.BlockSpec((bk, bn), lambda i, j, kk: (kk, j)),
                  pl.BlockSpec((bm, bn), lambda i, j, kk: (i, j))],
        out_specs=pl.BlockSpec((bm, bn), lambda i, j, kk: (i, j)),
        out_shape=jax.ShapeDtypeStruct((m, n), F32),
        compiler_params=_params("parallel", "arbitrary", "arbitrary"),
        name=name,
    )(x, w, res)


def _attn_kernel(lam_ref, q_ref, k_ref, v_ref, w_ref, o_ref, m_sc, l_sc, acc_sc, *, blk, nsub, lam_init):
    d = ATT_HEAD_DIM
    nt = blk // LANES
    qi = pl.program_id(2)

    m_sc[...] = jnp.full(m_sc.shape, -jnp.inf, F32)
    l_sc[...] = jnp.zeros(l_sc.shape, F32)
    acc_sc[...] = jnp.zeros(acc_sc.shape, F32)

    row = lax.broadcasted_iota(jnp.int32, (blk, blk), 0)
    col = lax.broadcasted_iota(jnp.int32, (blk, blk), 1)
    keep = col <= row

    def chain(sub, c, k, v, masked):
        ch = sub * 2 + c
        q = q_ref[sub * blk:(sub + 1) * blk, c * d:(c + 1) * d]
        s = _dot_nt(q, k[:, c * d:(c + 1) * d])
        if masked:
            s = jnp.where(keep, s, -jnp.inf)
        tiles = [s[:, j * LANES:(j + 1) * LANES] for j in range(nt)]
        tile_max = functools.reduce(jnp.maximum, tiles)
        m_prev = m_sc[ch]
        m_new = jnp.maximum(m_prev, jnp.max(tile_max, axis=-1, keepdims=True))
        alpha = jnp.exp2(m_prev - m_new)
        ps = [jnp.exp2(tl - m_new) for tl in tiles]
        l_sc[ch] = alpha * l_sc[ch] + functools.reduce(jnp.add, ps)
        m_sc[ch] = m_new
        pv = _dot(jnp.concatenate(ps, axis=1).astype(BF16), v)
        for j in range(pv.shape[1] // LANES):
            sl = slice(j * LANES, (j + 1) * LANES)
            acc_sc[ch, :, sl] = alpha * acc_sc[ch, :, sl] + pv[:, sl]

    def step(kb, first_sub, diag_sub):
        start = pl.multiple_of(kb * blk, blk)
        k = k_ref[pl.ds(start, blk), :]
        v = v_ref[pl.ds(start, blk), :]
        for sub in range(first_sub, nsub):
            for c in range(2):
                chain(sub, c, k, v, sub == diag_sub)

    def body(kb, carry):
        step(kb, 0, None)
        return carry

    lax.fori_loop(0, qi * nsub, body, 0)
    for sub in range(nsub):
        step(qi * nsub + sub, sub, sub)

    lv = lam_ref[...]
    a1 = jnp.sum(lv[0:1] * lv[1:2], axis=-1, keepdims=True)
    a2 = jnp.sum(lv[2:3] * lv[3:4], axis=-1, keepdims=True)
    lam = jnp.exp(a1) - jnp.exp(a2) + lam_init
    gain = w_ref[...] * (1.0 - lam_init)
    for sub in range(nsub):
        l1 = jnp.sum(l_sc[2 * sub], axis=-1, keepdims=True)
        l2 = jnp.sum(l_sc[2 * sub + 1], axis=-1, keepdims=True)
        o = acc_sc[2 * sub] / l1 - lam * (acc_sc[2 * sub + 1] / l2)
        ms = jnp.mean(o * o, axis=-1, keepdims=True)
        o_ref[sub * blk:(sub + 1) * blk, :] = (o * lax.rsqrt(ms + EPS) * gain).astype(o_ref.dtype)


def diff_attention(qk, v, lam_vecs, subln_w, lam_init, batch, seq, blk=512, nsub=4):
    t, width = v.shape
    hv = 2 * ATT_HEAD_DIM
    heads = width // hv
    blk = min(blk, seq)
    nsub = min(nsub, seq // blk)
    bq = blk * nsub
    nq = seq // bq
    kern = functools.partial(_attn_kernel, blk=blk, nsub=nsub, lam_init=lam_init)
    return pl.pallas_call(
        kern,
        grid=(batch, heads, nq),
        in_specs=[pl.BlockSpec((4, ATT_HEAD_DIM), lambda b, h, i: (0, 0)),
                  pl.BlockSpec((bq, hv), lambda b, h, i: (b * nq + i, h)),
                  pl.BlockSpec((seq, hv), lambda b, h, i: (b, heads + h)),
                  pl.BlockSpec((seq, hv), lambda b, h, i: (b, h)),
                  pl.BlockSpec((1, hv), lambda b, h, i: (0, 0))],
        out_specs=pl.BlockSpec((bq, hv), lambda b, h, i: (b * nq + i, h)),
        out_shape=jax.ShapeDtypeStruct((t, width), BF16),
        scratch_shapes=[pltpu.VMEM((2 * nsub, blk, LANES), F32),
                        pltpu.VMEM((2 * nsub, blk, LANES), F32),
                        pltpu.VMEM((2 * nsub, blk, hv), F32)],
        compiler_params=_params("parallel", "parallel", "arbitrary"),
        name="diff_attention",
    )(lam_vecs, qk, qk, v, subln_w.reshape(1, hv))


def _ssd_kernel(xbc_ref, z_ref, dt_ref, cw_ref, cb_ref, dtb_ref, alog_ref, dsk_ref, nw_ref,
                o_ref, ext_sc, state_sc, *, width):
    L = SSM_CHUNK
    n = SSM_STATE
    gw = width // SSM_GROUPS
    hpg = gw // SSM_HEAD_DIM
    pad = 8
    c = pl.program_id(1)

    @pl.when(c == 0)
    def _():
        ext_sc[0:pad, :] = jnp.zeros((pad, ext_sc.shape[1]), F32)
        state_sc[...] = jnp.zeros(state_sc.shape, F32)

    ext_sc[pad:pad + L, :] = xbc_ref[...]

    def conv_silu(lo, hi):
        a = ext_sc[:, lo:hi]
        acc = cb_ref[:, lo:hi] + a[pad:pad + L] * cw_ref[CONV_K - 1:CONV_K, lo:hi]
        for tap in range(CONV_K - 1):
            shifted = pltpu.roll(a, CONV_K - 1 - tap, axis=0)[pad:pad + L]
            acc = acc + shifted * cw_ref[tap:tap + 1, lo:hi]
        return _silu(acc)

    dtr = dt_ref[...] + dtb_ref[...]
    dtp = jnp.maximum(dtr, 0.0) + jnp.log1p(jnp.exp(-jnp.abs(dtr)))
    a = -jnp.exp(alog_ref[...])
    d_a = dtp * a
    row = lax.broadcasted_iota(jnp.int32, (L, L), 0)
    col = lax.broadcasted_iota(jnp.int32, (L, L), 1)
    causal = col <= row
    tril = jnp.where(causal, 1.0, 0.0).astype(BF16)
    hi_, mid_, lo_ = _split3(d_a)
    acum = _dot(tril, hi_) + _dot(tril, mid_) + _dot(tril, lo_)
    acum_t = acum.T
    dtp_t = dtp.T
    e_acum = jnp.exp(acum)
    a_last = acum[L - 1:L, :]
    to_end = jnp.exp(a_last - acum) * dtp
    e_last = jnp.exp(a_last)

    head_of_col = lax.broadcasted_iota(jnp.int32, (1, gw), 1) // SSM_HEAD_DIM

    def expand(x, g):
        out = x[:, g * hpg:g * hpg + 1]
        for hd in range(1, hpg):
            out = jnp.where(head_of_col == hd, x[:, g * hpg + hd:g * hpg + hd + 1], out)
        return jnp.broadcast_to(out, (x.shape[0], gw))

    for g in range(SSM_GROUPS):
        xg = conv_silu(g * gw, (g + 1) * gw)
        bg = conv_silu(width + g * n, width + (g + 1) * n)
        cg = conv_silu(width + SSM_GROUPS * n + g * n, width + SSM_GROUPS * n + (g + 1) * n)
        xg16 = xg.astype(BF16)
        cg16 = cg.astype(BF16)
        cb = _dot_nt(cg16, bg.astype(BF16))
        y = None
        for hd in range(hpg):
            h = g * hpg + hd
            seg = acum[:, h:h + 1] - acum_t[h:h + 1, :]
            decay = jnp.exp(jnp.where(causal, seg, -jnp.inf))
            w = cb * decay * dtp_t[h:h + 1, :]
            yh = _dot(w.astype(BF16), xg16)
            y = yh if y is None else jnp.where(head_of_col == hd, yh, y)
        st = state_sc[g]
        y = y + _dot(cg16, st.astype(BF16)) * expand(e_acum, g)
        xw = (xg * expand(to_end, g)).astype(BF16)
        state_sc[g] = st * expand(e_last, g) + _dot(bg.T.astype(BF16), xw)
        y = y + dsk_ref[:, g * gw:(g + 1) * gw] * xg
        gated = y * _silu(z_ref[:, g * gw:(g + 1) * gw])
        ms = jnp.mean(gated * gated, axis=-1, keepdims=True)
        o_ref[:, g * gw:(g + 1) * gw] = (gated * lax.rsqrt(ms + EPS)
                                         * nw_ref[:, g * gw:(g + 1) * gw]).astype(o_ref.dtype)

    ext_sc[0:pad, :] = xbc_ref[L - pad:L, :]


def ssd_mixer(cz, dt, conv_w, conv_b, dt_bias, a_log, d_skip, norm_w, batch, seq):
    t = cz.shape[0]
    conv_ch = conv_w.shape[1]
    width = norm_w.shape[0]
    heads = width // SSM_HEAD_DIM
    L = SSM_CHUNK
    nc = seq // L
    assert cz.shape[1] == conv_ch + width and conv_ch % width == 0
    z_blk = conv_ch // width
    pad_heads = lambda v: jnp.pad(v.reshape(1, heads), ((0, 0), (0, LANES - heads)))
    dsk = jnp.repeat(d_skip, SSM_HEAD_DIM).reshape(1, width)
    return pl.pallas_call(
        functools.partial(_ssd_kernel, width=width),
        grid=(batch, nc),
        in_specs=[pl.BlockSpec((L, conv_ch), lambda b, c: (b * nc + c, 0)),
                  pl.BlockSpec((L, width), lambda b, c: (b * nc + c, z_blk)),
                  pl.BlockSpec((L, LANES), lambda b, c: (b * nc + c, 0)),
                  pl.BlockSpec((CONV_K, conv_ch), lambda b, c: (0, 0)),
                  pl.BlockSpec((1, conv_ch), lambda b, c: (0, 0)),
                  pl.BlockSpec((1, LANES), lambda b, c: (0, 0)),
                  pl.BlockSpec((1, LANES), lambda b, c: (0, 0)),
                  pl.BlockSpec((1, width), lambda b, c: (0, 0)),
                  pl.BlockSpec((1, width), lambda b, c: (0, 0))],
        out_specs=pl.BlockSpec((L, width), lambda b, c: (b * nc + c, 0)),
        out_shape=jax.ShapeDtypeStruct((t, width), BF16),
        scratch_shapes=[pltpu.VMEM((8 + L, conv_ch), F32),
                        pltpu.VMEM((SSM_GROUPS, SSM_STATE, width // SSM_GROUPS), F32)],
        compiler_params=_params("parallel", "arbitrary"),
        name="ssd_mixer",
    )(cz, cz, dt, conv_w, conv_b.reshape(1, conv_ch), pad_heads(dt_bias), pad_heads(a_log), dsk,
      norm_w.reshape(1, width))


def _norm_router_kernel(x_ref, w_ref, rw_ref, h_ref, idx_ref, gate_ref, *, n_experts):
    x = x_ref[...]
    ms = jnp.mean(x * x, axis=-1, keepdims=True)
    hn = x * lax.rsqrt(ms + EPS) * w_ref[...]
    half = hn.shape[1] // 2
    hb = hn.astype(BF16).astype(F32)
    hi = pltpu.bitcast(hb[:, :half], jnp.uint32)
    lo = pltpu.bitcast(hb[:, half:], jnp.uint32)
    h_ref[...] = hi | (lo >> 16)
    hh, hm, _ = _split3(hn)
    rh, rm, _ = _split3(rw_ref[...])
    logits = _dot(hh, rh) + _dot(hh, rm) + _dot(hm, rh)
    lane = lax.broadcasted_iota(jnp.int32, logits.shape, 1)
    lanef = lane.astype(F32)
    logits = jnp.where(lane < n_experts, logits, -jnp.inf)
    m1 = jnp.max(logits, axis=-1, keepdims=True)
    i1 = jnp.min(jnp.where(logits == m1, lanef, float(LANES)), axis=-1, keepdims=True)
    rest = jnp.where(lanef == i1, -jnp.inf, logits)
    m2 = jnp.max(rest, axis=-1, keepdims=True)
    i2 = jnp.min(jnp.where(rest == m2, lanef, float(LANES)), axis=-1, keepdims=True)
    e = jnp.exp(m2 - m1)
    g1 = 1.0 / (1.0 + e)
    idx_ref[...] = jnp.where(lane == 0, i1, i2).astype(jnp.int32)
    gate_ref[...] = jnp.where(lane == 0, g1, e * g1)


def norm_router(x, w, router_w, bt=256):
    t, d = x.shape
    n_experts = router_w.shape[1]
    bt = min(bt, t)
    rw = jnp.pad(router_w, ((0, 0), (0, LANES - n_experts)))
    return pl.pallas_call(
        functools.partial(_norm_router_kernel, n_experts=n_experts),
        grid=(t // bt,),
        in_specs=[pl.BlockSpec((bt, d), lambda i: (i, 0)),
                  pl.BlockSpec((1, d), lambda i: (0, 0)),
                  pl.BlockSpec((d, LANES), lambda i: (0, 0))],
        out_specs=[pl.BlockSpec((bt, d // 2), lambda i: (i, 0)),
                   pl.BlockSpec((bt, LANES), lambda i: (i, 0)),
                   pl.BlockSpec((bt, LANES), lambda i: (i, 0))],
        out_shape=[jax.ShapeDtypeStruct((t, d // 2), jnp.uint32),
                   jax.ShapeDtypeStruct((t, LANES), jnp.int32),
                   jax.ShapeDtypeStruct((t, LANES), F32)],
        compiler_params=_params("parallel"),
        name="norm_router",
    )(x, w.reshape(1, d), rw)


def _issue_row_gather(idx_ref, n_rows, src_hbm, dst_ref, sem):
    for r in range(n_rows):
        pltpu.make_async_copy(src_hbm.at[pl.ds(idx_ref[0, 0, r], 1), :],
                              dst_ref.at[pl.ds(r, 1), :], sem).start()


def _wait_row_gather(n_rows, src_hbm, dst_ref, sem):
    pltpu.make_async_copy(src_hbm.at[pl.ds(0, n_rows), :], dst_ref, sem).wait()


def _moe_gather_kernel(tv_ref, tok_ref, tok_next_ref, h_hbm, o_ref, rows_sc, sems):
    i = pl.program_id(0)
    n = pl.num_programs(0)
    bm = o_ref.shape[0]
    slot = lax.rem(i, 2)

    @pl.when(jnp.logical_and(i == 0, tv_ref[0] != 0))
    def _():
        _issue_row_gather(tok_ref, bm, h_hbm, rows_sc.at[0], sems.at[0])

    nxt = jnp.minimum(i + 1, n - 1)

    @pl.when(jnp.logical_and(i + 1 < n, tv_ref[nxt] != 0))
    def _():
        _issue_row_gather(tok_next_ref, bm, h_hbm, rows_sc.at[1 - slot], sems.at[1 - slot])

    @pl.when(tv_ref[i] != 0)
    def _():
        _wait_row_gather(bm, h_hbm, rows_sc.at[slot], sems.at[slot])
        words = rows_sc[slot]
        half = words.shape[1]
        o_ref[:, :half] = pltpu.bitcast(words & jnp.uint32(0xFFFF0000), F32).astype(o_ref.dtype)
        o_ref[:, half:] = pltpu.bitcast(words << 16, F32).astype(o_ref.dtype)

    @pl.when(tv_ref[i] == 0)
    def _():
        o_ref[...] = jnp.zeros(o_ref.shape, o_ref.dtype)


def _expert_weights(te_ref, ci_ref, ne_ref, nc_ref, mi, bn, w_hbms, wf_scs, wb_scs, sems):
    j = pl.program_id(0)
    i = pl.program_id(1)
    nj = pl.num_programs(0)
    changed = jnp.logical_or(i == 0, te_ref[i] != te_ref[jnp.maximum(i - 1, 0)])

    def copies(e, jj):
        col = pl.multiple_of(jj * bn, bn)
        return [pltpu.make_async_copy(w.at[mi, e, :, pl.ds(col, bn)], wf, sem)
                for w, wf, sem in zip(w_hbms, wf_scs, sems)]

    @pl.when(changed)
    def _():
        @pl.when(j * nc_ref[0] + ci_ref[i] == 0)
        def _():
            for cp in copies(te_ref[0], 0):
                cp.start()

        for cp in copies(te_ref[i], j):
            cp.wait()
        for wf, wb in zip(wf_scs, wb_scs):
            wb[...] = wf[...].astype(BF16)

        more_here = ne_ref[i] >= 0

        @pl.when(more_here)
        def _():
            for cp in copies(ne_ref[i], j):
                cp.start()

        @pl.when(jnp.logical_and(jnp.logical_not(more_here), j + 1 < nj))
        def _():
            for cp in copies(te_ref[0], j + 1):
                cp.start()


def _moe_up_kernel(te_ref, tv_ref, ci_ref, ne_ref, nc_ref, x_ref, wg_hbm, wu_hbm, o_ref,
                   wgf_sc, wuf_sc, wgb_sc, wub_sc, sem_g, sem_u, *, mi):
    i = pl.program_id(1)
    valid = tv_ref[i] != 0
    _expert_weights(te_ref, ci_ref, ne_ref, nc_ref, mi, o_ref.shape[1], (wg_hbm, wu_hbm),
                    (wgf_sc, wuf_sc), (wgb_sc, wub_sc), (sem_g, sem_u))

    @pl.when(valid)
    def _():
        x = x_ref[...]
        g = _dot(x, wgb_sc[...])
        u = _dot(x, wub_sc[...])
        o_ref[...] = (_silu(g) * u).astype(o_ref.dtype)

    @pl.when(jnp.logical_not(valid))
    def _():
        o_ref[...] = jnp.zeros(o_ref.shape, o_ref.dtype)


def _moe_down_kernel(te_ref, tv_ref, ci_ref, ne_ref, nc_ref, h_ref, wd_hbm, o_ref,
                     wdf_sc, wdb_sc, sem_d, *, mi):
    i = pl.program_id(1)
    valid = tv_ref[i] != 0
    _expert_weights(te_ref, ci_ref, ne_ref, nc_ref, mi, o_ref.shape[1], (wd_hbm,),
                    (wdf_sc,), (wdb_sc,), (sem_d,))

    @pl.when(valid)
    def _():
        o_ref[...] = _dot(h_ref[...], wdb_sc[...])

    @pl.when(jnp.logical_not(valid))
    def _():
        o_ref[...] = jnp.zeros(o_ref.shape, o_ref.dtype)


def _moe_combine_kernel(pos_ref, pos_next_ref, x_ref, g_ref, y_hbm, o_ref, rows_sc, sems):
    i = pl.program_id(0)
    n = pl.num_programs(0)
    bt = x_ref.shape[0]
    slot = lax.rem(i, 2)

    @pl.when(i == 0)
    def _():
        _issue_row_gather(pos_ref, TOP_K * bt, y_hbm, rows_sc.at[0], sems.at[0])

    @pl.when(i + 1 < n)
    def _():
        _issue_row_gather(pos_next_ref, TOP_K * bt, y_hbm, rows_sc.at[1 - slot], sems.at[1 - slot])

    _wait_row_gather(TOP_K * bt, y_hbm, rows_sc.at[slot], sems.at[slot])
    g = g_ref[...]
    o_ref[...] = x_ref[...] + (g[:, 0:1] * rows_sc[slot, 0:bt, :] + g[:, 1:2] * rows_sc[slot, bt:2 * bt, :])


def moe_ffn(xt, norm_w, router_w, w_gate, w_up, w_down, mi, bn=512, bn_up=512, bt=256):
    t, d = xt.shape
    _, n_experts, _, f = w_gate.shape
    bm = min(MOE_TILE, t)
    bt = min(bt, t)
    h, idx, gates = norm_router(xt, norm_w, router_w)

    e_flat = idx[:, :TOP_K].reshape(-1)
    onehot = (e_flat[:, None] == jnp.arange(n_experts, dtype=jnp.int32)[None, :]).astype(jnp.int32)
    csum = jnp.cumsum(onehot, axis=0)
    rank = jnp.sum((csum - onehot) * onehot, axis=1)
    counts = csum[-1]
    tiles_e = (counts + bm - 1) // bm
    tile_end = jnp.cumsum(tiles_e)
    row_start = (tile_end - tiles_e) * bm
    pos = row_start[e_flat] + rank
    n_tiles = (TOP_K * t) // bm + n_experts
    n_rows = n_tiles * bm
    row_token = jnp.zeros((n_rows,), jnp.int32).at[pos].set(jnp.arange(TOP_K * t, dtype=jnp.int32) // TOP_K)
    tile_ids = jnp.arange(n_tiles, dtype=jnp.int32)
    n_valid = tile_end[-1]
    tile_valid = (tile_ids < n_valid).astype(jnp.int32)
    last_valid = jnp.minimum(tile_ids, n_valid - 1)
    tile_expert = jnp.sum((last_valid[:, None] >= tile_end[None, :]).astype(jnp.int32), axis=1)
    tile_expert = jnp.minimum(tile_expert, n_experts - 1)

    tok_tiles = row_token.reshape(n_tiles, 1, bm)
    xs = pl.pallas_call(
        _moe_gather_kernel,
        grid_spec=pltpu.PrefetchScalarGridSpec(
            num_scalar_prefetch=1,
            grid=(n_tiles,),
            in_specs=[pl.BlockSpec((1, 1, bm), lambda i, tv: (i, 0, 0), memory_space=pltpu.SMEM),
                      pl.BlockSpec((1, 1, bm), lambda i, tv: (jnp.minimum(i + 1, n_tiles - 1), 0, 0),
                                   memory_space=pltpu.SMEM),
                      pl.BlockSpec(memory_space=pl.ANY)],
            out_specs=pl.BlockSpec((bm, d), lambda i, tv: (i, 0)),
            scratch_shapes=[pltpu.VMEM((2, bm, d // 2), jnp.uint32), pltpu.SemaphoreType.DMA((2,))]),
        out_shape=jax.ShapeDtypeStruct((n_rows, d), BF16),
        compiler_params=_params("arbitrary"),
        name="moe_gather",
    )(tile_valid, tok_tiles, tok_tiles, h)

    owns = tiles_e > 0
    eids = jnp.arange(n_experts, dtype=jnp.int32)
    later = jnp.logical_and(owns[None, :], eids[None, :] > eids[:, None])
    next_owner = jnp.min(jnp.where(later, eids[None, :], n_experts), axis=1)
    next_owner = jnp.where(next_owner == n_experts, -1, next_owner).astype(jnp.int32)
    owner_idx = (jnp.cumsum(owns.astype(jnp.int32)) - owns.astype(jnp.int32)).astype(jnp.int32)
    tile_ci = owner_idx[tile_expert]
    tile_ne = next_owner[tile_expert]
    n_owners = jnp.sum(owns.astype(jnp.int32)).reshape(1)
    route = (tile_expert, tile_valid, tile_ci, tile_ne, n_owners)
    dma = pltpu.SemaphoreType.DMA(())

    hid = pl.pallas_call(
        functools.partial(_moe_up_kernel, mi=mi),
        grid_spec=pltpu.PrefetchScalarGridSpec(
            num_scalar_prefetch=len(route),
            grid=(f // bn_up, n_tiles),
            in_specs=[pl.BlockSpec((bm, d), lambda j, i, *_: (i, 0)),
                      pl.BlockSpec(memory_space=pl.ANY),
                      pl.BlockSpec(memory_space=pl.ANY)],
            out_specs=pl.BlockSpec((bm, bn_up), lambda j, i, *_: (i, j)),
            scratch_shapes=[pltpu.VMEM((d, bn_up), F32), pltpu.VMEM((d, bn_up), F32),
                            pltpu.VMEM((d, bn_up), BF16), pltpu.VMEM((d, bn_up), BF16), dma, dma]),
        out_shape=jax.ShapeDtypeStruct((n_rows, f), BF16),
        compiler_params=_params("arbitrary", "arbitrary"),
        name="moe_up",
    )(*route, xs, w_gate, w_up)

    y = pl.pallas_call(
        functools.partial(_moe_down_kernel, mi=mi),
        grid_spec=pltpu.PrefetchScalarGridSpec(
            num_scalar_prefetch=len(route),
            grid=(d // bn, n_tiles),
            in_specs=[pl.BlockSpec((bm, f), lambda j, i, *_: (i, 0)),
                      pl.BlockSpec(memory_space=pl.ANY)],
            out_specs=pl.BlockSpec((bm, bn), lambda j, i, *_: (i, j)),
            scratch_shapes=[pltpu.VMEM((f, bn), F32), pltpu.VMEM((f, bn), BF16), dma]),
        out_shape=jax.ShapeDtypeStruct((n_rows, d), F32),
        compiler_params=_params("arbitrary", "arbitrary"),
        name="moe_down",
    )(*route, hid, w_down)

    pos_tiles = pos.reshape(t // bt, bt, TOP_K).transpose(0, 2, 1).reshape(t // bt, 1, TOP_K * bt)
    return pl.pallas_call(
        _moe_combine_kernel,
        grid=(t // bt,),
        in_specs=[pl.BlockSpec((1, 1, TOP_K * bt), lambda i: (i, 0, 0), memory_space=pltpu.SMEM),
                  pl.BlockSpec((1, 1, TOP_K * bt), lambda i: (jnp.minimum(i + 1, t // bt - 1), 0, 0),
                               memory_space=pltpu.SMEM),
                  pl.BlockSpec((bt, d), lambda i: (i, 0)),
                  pl.BlockSpec((bt, LANES), lambda i: (i, 0)),
                  pl.BlockSpec(memory_space=pl.ANY)],
        out_specs=pl.BlockSpec((bt, d), lambda i: (i, 0)),
        out_shape=jax.ShapeDtypeStruct((t, d), F32),
        scratch_shapes=[pltpu.VMEM((2, TOP_K * bt, d), F32), pltpu.SemaphoreType.DMA((2,))],
        compiler_params=_params("arbitrary"),
        name="moe_combine",
    )(pos_tiles, pos_tiles, xt, gates, y)


def _lambda_init(layer):
    return 0.8 - 0.6 * math.exp(-0.3 * layer)


def kernel(x, norm1_w, w_in, w_out, q_norm_w, k_norm_w, lambda_q1, lambda_k1, lambda_q2, lambda_k2,
           subln_w, conv_w, conv_b, dt_bias, a_log, d_skip, ssm_norm_w, norm2_w, ffn_w_gate, ffn_w_up,
           ffn_w_down, router_w, moe_w_gate, moe_w_up, moe_w_down):
    b, s, d = x.shape
    t = b * s
    depth = w_in.shape[0]
    ssm_width = ssm_norm_w.shape[1]
    conv_ch = conv_w.shape[2]
    ssm_heads = dt_bias.shape[1]
    att_width = (w_in.shape[2] - ssm_width - conv_ch - ssm_heads) // 3
    n_qk_heads = att_width // ATT_HEAD_DIM
    q_scale = ATT_HEAD_DIM ** -0.5 * math.log2(math.e)
    bn = 512
    z0 = 3 * att_width
    c0 = z0 + ssm_width
    nz, nx = ssm_width // bn, conv_ch // bn

    xt = x.reshape(t, d)
    w_in_t = jnp.swapaxes(w_in, 1, 2)
    for layer in range(depth):

        h = rmsnorm(xt, norm1_w[layer])
        gain = jnp.concatenate([jnp.tile(q_norm_w[layer] * q_scale, n_qk_heads),
                                jnp.tile(k_norm_w[layer], n_qk_heads)]).reshape(1, 2 * att_width)
        qk = matmul_ws(h, w_in_t, layer, 0, 2 * att_width, BF16, gain=gain, bn=bn, name="proj_qk")
        v = matmul_ws(h, w_in_t, layer, 2 * att_width, att_width, BF16, bn=bn, name="proj_v")
        cz = matmul_ws(h, w_in_t, layer, z0, ssm_width + conv_ch, F32, bn=bn, name="proj_ssm",
                       out_block=lambda j: jnp.where(j < nz, j + nx, j - nz))
        dt = matmul_narrow(h, w_in_t, layer, c0 + conv_ch, ssm_heads, name="proj_dt")

        lam_vecs = jnp.stack([lambda_q1[layer], lambda_k1[layer], lambda_q2[layer], lambda_k2[layer]])
        att = diff_attention(qk, v, lam_vecs, subln_w[layer], _lambda_init(layer), b, s)
        ssm = ssd_mixer(cz, dt, conv_w[layer], conv_b[layer], dt_bias[layer], a_log[layer],
                        d_skip[layer], ssm_norm_w[layer], b, s)
        xt = matmul_ws_res2(att, ssm, w_out, layer, xt, bn=bn)

        i = layer // 2
        if layer % 2 == 0:
            f = ffn_w_gate.shape[2]
            h2 = rmsnorm(xt, norm2_w[layer])
            hid = swiglu_up_ws(h2, ffn_w_gate, ffn_w_up, i)
            xt = matmul_residual_ktiled(hid, ffn_w_down[i].astype(BF16), xt, bk=f // 2, name="ffn_down")
        else:
            xt = moe_ffn(xt, norm2_w[layer], router_w[i], moe_w_gate, moe_w_up, moe_w_down, i)
    return xt.reshape(b, s, d)
```
